```python
import jax, jax.numpy as jnp
from jax import lax
import numpy as np

D_MODEL = 1024
BATCH = 4
SEQ = 8192
DEPTH = 1

CHUNK = 64
HEAD_DIM = 64
N_HEADS_A = 8
N_HEADS_B = 8
D_A = N_HEADS_A * HEAD_DIM
D_B = N_HEADS_B * HEAD_DIM
D_MIX = D_A + D_B
N_LEFT_CHUNKS = 8
BAND_CHUNKS = N_LEFT_CHUNKS + 1
REL_CLIP = 128
Q_BLOCK = 128
D_FF = 2816
CONV_WIDTH = 3
LN_EPS = 1e-5
DEEPNORM_ALPHA = (2.0 * DEPTH) ** 0.25
DEEPNORM_BETA = (8.0 * DEPTH) ** -0.25
PROJ_COLS = 3 * D_A + N_HEADS_A + 3 * D_B
SPLITS = (D_A, 2 * D_A, 3 * D_A, 3 * D_A + N_HEADS_A,
          3 * D_A + N_HEADS_A + D_B, 3 * D_A + N_HEADS_A + 2 * D_B)

kernel_name = "hymba_style_fox_chunkband_convffn_deepnorm"


def layer_norm(x, g, b):
    xf = x.astype(jnp.float32)
    mu = jnp.mean(xf, axis=-1, keepdims=True)
    var = jnp.mean(jnp.square(xf - mu), axis=-1, keepdims=True)
    y = (xf - mu) * lax.rsqrt(var + LN_EPS)
    return (y * g.astype(jnp.float32) + b.astype(jnp.float32)).astype(x.dtype)


def split_heads(t, n_heads):
    b, s, _ = t.shape
    return t.reshape(b, s, n_heads, HEAD_DIM).transpose(0, 2, 1, 3)


def forgetting_attention(q, k, v, log_f):
    b, h, s, d = q.shape
    n_blk = s // Q_BLOCK
    cum = jnp.cumsum(log_f.astype(jnp.float32), axis=-1)
    q_blocks = q.reshape(b, h, n_blk, Q_BLOCK, d).transpose(2, 0, 1, 3, 4)
    c_blocks = cum.reshape(b, h, n_blk, Q_BLOCK).transpose(2, 0, 1, 3)
    starts = jnp.arange(n_blk, dtype=jnp.int32) * Q_BLOCK
    k_pos = jnp.arange(s, dtype=jnp.int32)
    scale = d ** -0.5

    def one_block(args):
        qb, cb, start = args
        logits = jnp.einsum('bhqd,bhkd->bhqk', qb, k).astype(jnp.float32) * scale
        logits = logits + cb[..., :, None] - cum[..., None, :]
        q_pos = start + jnp.arange(Q_BLOCK, dtype=jnp.int32)
        causal = k_pos[None, :] <= q_pos[:, None]
        logits = jnp.where(causal, logits, -jnp.inf)
        p = jax.nn.softmax(logits, axis=-1)
        return jnp.einsum('bhqk,bhkd->bhqd', p.astype(v.dtype), v)

    out = lax.map(one_block, (q_blocks, c_blocks, starts))
    return out.transpose(1, 2, 0, 3, 4).reshape(b, h, s, d)


def chunked_band_attention(q, k, v, rel_bias):
    b, h, s, d = q.shape
    n_c = s // CHUNK
    band = BAND_CHUNKS * CHUNK
    qc = q.reshape(b, h, n_c, CHUNK, d)
    pad = ((0, 0), (0, 0), (N_LEFT_CHUNKS, 0), (0, 0), (0, 0))
    kp = jnp.pad(k.reshape(b, h, n_c, CHUNK, d), pad)
    vp = jnp.pad(v.reshape(b, h, n_c, CHUNK, d), pad)
    band_idx = np.arange(n_c)[:, None] + np.arange(BAND_CHUNKS)[None, :]
    k_band = kp[:, :, band_idx].reshape(b, h, n_c, band, d)
    v_band = vp[:, :, band_idx].reshape(b, h, n_c, band, d)
    dist = (N_LEFT_CHUNKS * CHUNK + np.arange(CHUNK)[:, None]) - np.arange(band)[None, :]
    rel_idx = np.clip(dist, -REL_CLIP, REL_CLIP) + REL_CLIP
    bias = rel_bias.astype(jnp.float32)[:, rel_idx]
    valid = (np.arange(n_c)[:, None] + np.arange(band)[None, :] // CHUNK) >= N_LEFT_CHUNKS
    logits = jnp.einsum('bhcqd,bhckd->bhcqk', qc, k_band).astype(jnp.float32) * (d ** -0.5)
    logits = logits + bias[None, :, None, :, :]
    logits = jnp.where(valid[None, None, :, None, :], logits, -jnp.inf)
    p = jax.nn.softmax(logits, axis=-1)
    out = jnp.einsum('bhcqk,bhckd->bhcqd', p.astype(v.dtype), v_band)
    return out.reshape(b, h, s, d)


def token_mixer(x, w_in, b_forget, rel_bias, w_out):
    b, s, _ = x.shape
    proj = x @ w_in
    q_a, k_a, v_a, f_a, q_b, k_b, v_b = jnp.split(proj, SPLITS, axis=-1)
    log_f = jax.nn.log_sigmoid((f_a + b_forget).astype(jnp.float32)).transpose(0, 2, 1)
    y_a = forgetting_attention(split_heads(q_a, N_HEADS_A), split_heads(k_a, N_HEADS_A),
                               split_heads(v_a, N_HEADS_A), log_f)
    y_b = chunked_band_attention(split_heads(q_b, N_HEADS_B), split_heads(k_b, N_HEADS_B),
                                 split_heads(v_b, N_HEADS_B), rel_bias)
    y = jnp.concatenate([y_a, y_b], axis=1)
    y = y.transpose(0, 2, 1, 3).reshape(b, s, D_MIX)
    return y @ w_out


def conv_ffn(x, w_up, conv_w, conv_b, w_down):
    s = x.shape[1]
    u = x @ w_up
    u_pad = jnp.pad(u, ((0, 0), (CONV_WIDTH - 1, 0), (0, 0)))
    u = sum(u_pad[:, j:j + s, :] * conv_w[j] for j in range(CONV_WIDTH)) + conv_b
    value, gate = jnp.split(u, 2, axis=-1)
    return (value * jax.nn.gelu(gate)) @ w_down


def setup_inputs(seed: int = 0) -> dict:
    key = jax.random.key(seed)
    ks = jax.random.split(key, 16)
    f32 = jnp.float32
    x = jax.random.normal(ks[0], (BATCH, SEQ, D_MODEL), f32)
    col_scale = np.ones((PROJ_COLS,), np.float32)
    col_scale[2 * D_A:3 * D_A] = DEEPNORM_BETA
    col_scale[SPLITS[5]:] = DEEPNORM_BETA
    w_in = jax.random.normal(ks[1], (DEPTH, D_MODEL, PROJ_COLS), f32) * (D_MODEL ** -0.5) * jnp.asarray(col_scale)
    b_forget = jax.random.uniform(ks[2], (DEPTH, N_HEADS_A), f32, 2.0, 5.0)
    rel_bias = jax.random.normal(ks[3], (DEPTH, N_HEADS_B, 2 * REL_CLIP + 1), f32) * 0.5
    w_out = jax.random.normal(ks[4], (DEPTH, D_MIX, D_MODEL), f32) * (D_MIX ** -0.5) * DEEPNORM_BETA
    ln1_g = 1.0 + 0.05 * jax.random.normal(ks[5], (DEPTH, D_MODEL), f32)
    ln1_b = 0.02 * jax.random.normal(ks[6], (DEPTH, D_MODEL), f32)
    w_up = jax.random.normal(ks[7], (DEPTH, D_MODEL, 2 * D_FF), f32) * (D_MODEL ** -0.5)
    conv_w = jax.random.normal(ks[8], (DEPTH, CONV_WIDTH, 2 * D_FF), f32) * (CONV_WIDTH ** -0.5)
    conv_b = 0.02 * jax.random.normal(ks[9], (DEPTH, 2 * D_FF), f32)
    w_down = jax.random.normal(ks[10], (DEPTH, D_FF, D_MODEL), f32) * (D_FF ** -0.5) * DEEPNORM_BETA
    ln2_g = 1.0 + 0.05 * jax.random.normal(ks[11], (DEPTH, D_MODEL), f32)
    ln2_b = 0.02 * jax.random.normal(ks[12], (DEPTH, D_MODEL), f32)
    return {"x": x, "w_in": w_in, "b_forget": b_forget, "rel_bias": rel_bias, "w_out": w_out,
            "ln1_g": ln1_g, "ln1_b": ln1_b, "w_up": w_up, "conv_w": conv_w, "conv_b": conv_b,
            "w_down": w_down, "ln2_g": ln2_g, "ln2_b": ln2_b}


def reference(x, w_in, b_forget, rel_bias, w_out, ln1_g, ln1_b, w_up, conv_w, conv_b,
              w_down, ln2_g, ln2_b):
    for l in range(DEPTH):
        mix = token_mixer(x, w_in[l], b_forget[l], rel_bias[l], w_out[l])
        x = layer_norm(DEEPNORM_ALPHA * x + mix, ln1_g[l], ln1_b[l])
        ffn = conv_ffn(x, w_up[l], conv_w[l], conv_b[l], w_down[l])
        x = layer_norm(DEEPNORM_ALPHA * x + ffn, ln2_g[l], ln2_b[l])
    return x
```

```python
import functools
import math

import jax
import jax.numpy as jnp
import numpy as np
from jax import lax
from jax.experimental import pallas as pl
from jax.experimental.pallas import tpu as pltpu

D_MODEL = 1024
HEAD_DIM = 64
N_HEADS_A = 8
N_HEADS_B = 8
D_A = N_HEADS_A * HEAD_DIM
D_B = N_HEADS_B * HEAD_DIM
CHUNK = 64
N_LEFT_CHUNKS = 8
REL_CLIP = 128
D_FF = 2816
LN_EPS = 1e-5
DEPTH = 1
DEEPNORM_ALPHA = (2.0 * DEPTH) ** 0.25

LANES = 128
PAIR = 2 * HEAD_DIM
N_PAIRS_A = D_A // PAIR
N_PAIRS_B = D_B // PAIR
ONES_ROWS = 16
NEG = -1e30
VMEM_LIMIT = 52 * 1024 * 1024

PROJ_TM = 512
GATE_TB = 512
FOX_T = 512
BAND_TQ = 512
BAND_G = 256
BAND_KW = BAND_G + N_LEFT_CHUNKS * CHUNK
MIX_TM = 512
FFN_TM = 256
FFN_CW = 256

_NT = (((1,), (1,)), ((), ()))


def _params(sem):
    return pltpu.CompilerParams(dimension_semantics=sem, vmem_limit_bytes=VMEM_LIMIT)


def _proj_kernel(x_ref, wqk_ref, wvt_ref, wf_ref, qk_ref, vt_ref, f_ref):
    xb = x_ref[...].astype(jnp.bfloat16)
    qk = jnp.dot(xb, wqk_ref[...], preferred_element_type=jnp.float32)
    scale = HEAD_DIM ** -0.5
    col = lax.broadcasted_iota(jnp.int32, (1, qk.shape[1]), 1)
    is_q = (col < D_A) | ((col >= 2 * D_A) & (col < 2 * D_A + D_B))
    qk_ref[...] = (qk * jnp.where(is_q, scale, 1.0)).astype(jnp.bfloat16)
    vt = lax.dot_general(wvt_ref[...], xb, _NT, preferred_element_type=jnp.float32)
    vt_ref[...] = vt.astype(jnp.bfloat16)
    f_ref[...] = jnp.dot(xb, wf_ref[...], preferred_element_type=jnp.float32)


def _proj(x, wqk, wvt, wf):
    b, s, d = x.shape
    tm = PROJ_TM
    nqk = wqk.shape[1]
    nv = wvt.shape[0]
    return pl.pallas_call(
        _proj_kernel,
        grid=(b, s // tm),
        in_specs=[
            pl.BlockSpec((None, tm, d), lambda bi, i: (bi, i, 0)),
            pl.BlockSpec((d, nqk), lambda bi, i: (0, 0)),
            pl.BlockSpec((nv, d), lambda bi, i: (0, 0)),
            pl.BlockSpec((d, LANES), lambda bi, i: (0, 0)),
        ],
        out_specs=[
            pl.BlockSpec((None, tm, nqk), lambda bi, i: (bi, i, 0)),
            pl.BlockSpec((None, nv, tm), lambda bi, i: (bi, 0, i)),
            pl.BlockSpec((None, tm, LANES), lambda bi, i: (bi, i, 0)),
        ],
        out_shape=[
            jax.ShapeDtypeStruct((b, s, nqk), jnp.bfloat16),
            jax.ShapeDtypeStruct((b, nv, s), jnp.bfloat16),
            jax.ShapeDtypeStruct((b, s, LANES), jnp.float32),
        ],
        compiler_params=_params(("parallel", "parallel")),
        name="proj",
    )(x, wqk, wvt, wf)


def _split3(v):
    h1 = v.astype(jnp.bfloat16).astype(jnp.float32)
    r1 = v - h1
    h2 = r1.astype(jnp.bfloat16).astype(jnp.float32)
    h3 = r1 - h2
    return h1, h2, h3


def _gate_kernel(f_ref, b_ref, ka_ref, carry_ref):
    i = pl.program_id(1)

    @pl.when(i == 0)
    def _():
        carry_ref[...] = jnp.zeros_like(carry_ref)

    z = f_ref[...] + b_ref[...]
    logf = -(jnp.maximum(-z, 0.0) + jnp.log1p(jnp.exp(-jnp.abs(z))))
    tb = z.shape[0]
    r = lax.broadcasted_iota(jnp.int32, (tb, tb), 0)
    c = lax.broadcasted_iota(jnp.int32, (tb, tb), 1)
    tri = jnp.where(c <= r, 1.0, 0.0).astype(jnp.bfloat16)
    cum = carry_ref[0:1, :]
    for part in _split3(logf):
        cum = cum + jnp.dot(tri, part.astype(jnp.bfloat16), preferred_element_type=jnp.float32)
    carry_ref[...] = jnp.broadcast_to(cum[tb - 1:tb, :], carry_ref.shape)
    neg = -cum
    lane = lax.broadcasted_iota(jnp.int32, (tb, LANES), 1)
    for hp in range(N_PAIRS_A):
        out = jnp.zeros((tb, LANES), jnp.float32)
        for j in range(2):
            parts = _split3(neg[:, 2 * hp + j:2 * hp + j + 1])
            for t, p in enumerate(parts):
                out = jnp.where(lane == 3 * j + t, p, out)
        ka_ref[hp] = out.astype(jnp.bfloat16)


def _gate(f, b_row):
    b, s, _ = f.shape
    tb = GATE_TB
    return pl.pallas_call(
        _gate_kernel,
        grid=(b, s // tb),
        in_specs=[
            pl.BlockSpec((None, tb, LANES), lambda bi, i: (bi, i, 0)),
            pl.BlockSpec((1, LANES), lambda bi, i: (0, 0)),
        ],
        out_specs=pl.BlockSpec((None, N_PAIRS_A, tb, LANES), lambda bi, i: (bi, 0, i, 0)),
        out_shape=jax.ShapeDtypeStruct((b, N_PAIRS_A, s, LANES), jnp.bfloat16),
        scratch_shapes=[pltpu.VMEM((8, LANES), jnp.float32)],
        compiler_params=_params(("parallel", "arbitrary")),
        name="gate",
    )(f, b_row)


def _split_heads(q2):
    lane = lax.broadcasted_iota(jnp.int32, q2.shape, 1)
    qf = q2.astype(jnp.float32)
    even = jnp.where(lane < HEAD_DIM, qf, 0.0).astype(jnp.bfloat16)
    odd = jnp.where(lane < HEAD_DIM, 0.0, qf).astype(jnp.bfloat16)
    return even, odd


def _finalize(acc_e, acc_o):
    oe = acc_e[0:HEAD_DIM, :] / acc_e[PAIR:PAIR + 1, :]
    oo = acc_o[HEAD_DIM:PAIR, :] / acc_o[PAIR:PAIR + 1, :]
    return jnp.concatenate([oe, oo], axis=0).T


def _fox_kernel(q_ref, k_ref, ka_ref, vt_ref, o_ref, qe_ref, qo_ref, acc_e, acc_o, m_e, m_o):
    qi = pl.program_id(2)
    t = FOX_T
    q2 = q_ref[...]
    lane = lax.broadcasted_iota(jnp.int32, q2.shape, 1)
    qe_ref[:, 0:LANES], qo_ref[:, 0:LANES] = _split_heads(q2)
    qe_ref[:, LANES:] = jnp.where(lane < 3, 1.0, 0.0).astype(jnp.bfloat16)
    qo_ref[:, LANES:] = jnp.where((lane >= 3) & (lane < 6), 1.0, 0.0).astype(jnp.bfloat16)
    acc_e[...] = jnp.zeros_like(acc_e)
    acc_o[...] = jnp.zeros_like(acc_o)
    m_e[...] = jnp.full_like(m_e, NEG)
    m_o[...] = jnp.full_like(m_o, NEG)
    ones_rows = jnp.ones((ONES_ROWS, t), jnp.bfloat16)

    def block(ki, masked):
        ks = pl.multiple_of(ki * t, t)
        kcat = jnp.concatenate([k_ref[pl.ds(ks, t), :], ka_ref[pl.ds(ks, t), :]], axis=1)
        vt = jnp.concatenate([vt_ref[:, pl.ds(ks, t)], ones_rows], axis=0)
        for qx, acc, m in ((qe_ref, acc_e, m_e), (qo_ref, acc_o, m_o)):
            st = lax.dot_general(kcat, qx[...], _NT, preferred_element_type=jnp.float32)
            if masked:
                kr = lax.broadcasted_iota(jnp.int32, st.shape, 0)
                qc = lax.broadcasted_iota(jnp.int32, st.shape, 1)
                st = jnp.where(kr <= qc, st, NEG)
            m_old = m[...]
            m_new = jnp.maximum(m_old, jnp.max(st, axis=0, keepdims=True))
            p = jnp.exp(st - m_new).astype(jnp.bfloat16)
            acc[...] = acc[...] * jnp.exp(m_old - m_new) + jnp.dot(vt, p, preferred_element_type=jnp.float32)
            m[...] = m_new

    def body(ki, carry):
        block(ki, False)
        return carry

    lax.fori_loop(0, qi, body, 0)
    block(qi, True)
    o_ref[...] = _finalize(acc_e[...], acc_o[...]).astype(o_ref.dtype)


def _fox(qk, kaug, vt):
    b, s, _ = qk.shape
    t = FOX_T
    kcol0 = D_A // LANES
    return pl.pallas_call(
        _fox_kernel,
        grid=(b, N_PAIRS_A, s // t),
        in_specs=[
            pl.BlockSpec((None, t, LANES), lambda bi, hp, qi: (bi, qi, hp)),
            pl.BlockSpec((None, s, LANES), lambda bi, hp, qi: (bi, 0, kcol0 + hp)),
            pl.BlockSpec((None, None, s, LANES), lambda bi, hp, qi: (bi, hp, 0, 0)),
            pl.BlockSpec((None, LANES, s), lambda bi, hp, qi: (bi, hp, 0)),
        ],
        out_specs=pl.BlockSpec((None, t, LANES), lambda bi, hp, qi: (bi, qi, hp)),
        out_shape=jax.ShapeDtypeStruct((b, s, D_A), jnp.bfloat16),
        scratch_shapes=[
            pltpu.VMEM((t, 2 * LANES), jnp.bfloat16),
            pltpu.VMEM((t, 2 * LANES), jnp.bfloat16),
            pltpu.VMEM((PAIR + ONES_ROWS, t), jnp.float32),
            pltpu.VMEM((PAIR + ONES_ROWS, t), jnp.float32),
            pltpu.VMEM((1, t), jnp.float32),
            pltpu.VMEM((1, t), jnp.float32),
        ],
        compiler_params=_params(("parallel", "parallel", "arbitrary")),
        name="fox",
    )(qk, qk, kaug, vt)


def _band_kernel(q_ref, kp_ref, kc_ref, vtp_ref, vtc_ref, bias_ref, o_ref, kwin, vtwin):
    i = pl.program_id(2)
    tq = BAND_TQ
    kwin[0:tq, :] = kp_ref[...]
    kwin[tq:2 * tq, :] = kc_ref[...]
    vtwin[0:PAIR, 0:tq] = vtp_ref[...]
    vtwin[0:PAIR, tq:2 * tq] = vtc_ref[...]
    vtwin[PAIR:, :] = jnp.ones((ONES_ROWS, 2 * tq), jnp.bfloat16)
    for g in range(tq // BAND_G):
        q2 = q_ref[g * BAND_G:(g + 1) * BAND_G, :]
        kslab = kwin[g * BAND_G:g * BAND_G + BAND_KW, :]
        vslab = vtwin[:, g * BAND_G:g * BAND_G + BAND_KW]
        kpos = (i - 1) * tq + g * BAND_G + lax.broadcasted_iota(jnp.int32, (BAND_KW, BAND_G), 0)
        accs = []
        for h, qh in enumerate(_split_heads(q2)):
            st = lax.dot_general(kslab, qh, _NT, preferred_element_type=jnp.float32)
            st = jnp.where(kpos >= 0, st + bias_ref[h], NEG)
            m = jnp.max(st, axis=0, keepdims=True)
            p = jnp.exp(st - m).astype(jnp.bfloat16)
            accs.append(jnp.dot(vslab, p, preferred_element_type=jnp.float32))
        o_ref[g * BAND_G:(g + 1) * BAND_G, :] = _finalize(accs[0], accs[1]).astype(o_ref.dtype)


def _band(qk, vt, bias_t):
    b, s, _ = qk.shape
    tq = BAND_TQ
    qcol0 = 2 * D_A // LANES
    kcol0 = (2 * D_A + D_B) // LANES
    vrow0 = D_A // PAIR
    prev = lambda i: jnp.maximum(i - 1, 0)
    return pl.pallas_call(
        _band_kernel,
        grid=(b, N_PAIRS_B, s // tq),
        in_specs=[
            pl.BlockSpec((None, tq, LANES), lambda bi, hp, i: (bi, i, qcol0 + hp)),
            pl.BlockSpec((None, tq, LANES), lambda bi, hp, i: (bi, prev(i), kcol0 + hp)),
            pl.BlockSpec((None, tq, LANES), lambda bi, hp, i: (bi, i, kcol0 + hp)),
            pl.BlockSpec((None, PAIR, tq), lambda bi, hp, i: (bi, vrow0 + hp, prev(i))),
            pl.BlockSpec((None, PAIR, tq), lambda bi, hp, i: (bi, vrow0 + hp, i)),
            pl.BlockSpec((2, BAND_KW, BAND_G), lambda bi, hp, i: (hp, 0, 0)),
        ],
        out_specs=pl.BlockSpec((None, tq, LANES), lambda bi, hp, i: (bi, i, hp)),
        out_shape=jax.ShapeDtypeStruct((b, s, D_B), jnp.bfloat16),
        scratch_shapes=[
            pltpu.VMEM((2 * tq, LANES), jnp.bfloat16),
            pltpu.VMEM((PAIR + ONES_ROWS, 2 * tq), jnp.bfloat16),
        ],
        compiler_params=_params(("parallel", "parallel", "arbitrary")),
        name="band",
    )(qk, qk, qk, vt, vt, bias_t)


def _band_bias(rel_bias):
    r = np.arange(BAND_KW)[:, None]
    c = np.arange(BAND_G)[None, :]
    dist = N_LEFT_CHUNKS * CHUNK + c - r
    idx = np.clip(dist, -REL_CLIP, REL_CLIP) + REL_CLIP
    kc, qc = r // CHUNK, c // CHUNK
    inband = (kc >= qc) & (kc <= qc + N_LEFT_CHUNKS)
    return jnp.where(jnp.asarray(inband)[None], rel_bias.astype(jnp.float32)[:, idx], NEG)


def _layer_norm(z, g, b):
    mu = jnp.mean(z, axis=-1, keepdims=True)
    zc = z - mu
    var = jnp.mean(zc * zc, axis=-1, keepdims=True)
    return zc * lax.rsqrt(var + LN_EPS) * g + b


def _mix_kernel(ya_ref, yb_ref, x_ref, w_ref, g_ref, b_ref, o_ref):
    y = jnp.concatenate([ya_ref[...], yb_ref[...]], axis=1)
    mix = jnp.dot(y, w_ref[...], preferred_element_type=jnp.float32)
    o_ref[...] = _layer_norm(DEEPNORM_ALPHA * x_ref[...] + mix, g_ref[...], b_ref[...])


def _mix(ya, yb, x, w_out, g, b):
    m, d = x.shape
    tm = MIX_TM
    row = pl.BlockSpec((1, d), lambda i: (0, 0))
    return pl.pallas_call(
        _mix_kernel,
        grid=(m // tm,),
        in_specs=[
            pl.BlockSpec((tm, ya.shape[1]), lambda i: (i, 0)),
            pl.BlockSpec((tm, yb.shape[1]), lambda i: (i, 0)),
            pl.BlockSpec((tm, d), lambda i: (i, 0)),
            pl.BlockSpec((d, d), lambda i: (0, 0)),
            row, row,
        ],
        out_specs=pl.BlockSpec((tm, d), lambda i: (i, 0)),
        out_shape=jax.ShapeDtypeStruct((m, d), jnp.float32),
        compiler_params=_params(("parallel",)),
        name="mix",
    )(ya, yb, x, w_out, g, b)


def _gelu_tanh(x):
    return 0.5 * x * (1.0 + jnp.tanh(math.sqrt(2.0 / math.pi) * (x + 0.044715 * (x * x * x))))


def _causal_conv(u, carry, w, bias):
    tm = u.shape[0]
    r8 = lax.broadcasted_iota(jnp.int32, (8, u.shape[1]), 0)
    outs = w[2:3, :] * u + bias
    for shift, j in ((1, 1), (2, 0)):
        ur = pltpu.roll(u, shift, 0)
        cr = pltpu.roll(carry, shift, 0)
        head = jnp.where(r8 < shift, cr, ur[0:8, :])
        us = jnp.concatenate([head, ur[8:tm, :]], axis=0)
        outs = outs + w[j:j + 1, :] * us
    return outs


def _ffn_kernel(x_ref, wup_ref, cw_ref, cb_ref, wdn_ref, g_ref, b_ref, o_ref, carry_ref, h_ref):
    i = pl.program_id(1)

    @pl.when(i == 0)
    def _():
        carry_ref[...] = jnp.zeros_like(carry_ref)

    x = x_ref[...]
    xb = x.astype(jnp.bfloat16)
    tm = x.shape[0]
    for c in range(D_FF // FFN_CW):
        halves = []
        for off in (c * FFN_CW, D_FF + c * FFN_CW):
            sl = slice(off, off + FFN_CW)
            u = jnp.dot(xb, wup_ref[:, sl], preferred_element_type=jnp.float32)
            halves.append(_causal_conv(u, carry_ref[:, sl], cw_ref[:, sl], cb_ref[:, sl]))
            carry_ref[:, sl] = u[tm - 8:tm, :]
        h_ref[:, c * FFN_CW:(c + 1) * FFN_CW] = (halves[0] * _gelu_tanh(halves[1])).astype(jnp.bfloat16)
    ffn = jnp.dot(h_ref[...], wdn_ref[...], preferred_element_type=jnp.float32)
    o_ref[...] = _layer_norm(DEEPNORM_ALPHA * x + ffn, g_ref[...], b_ref[...])


def _ffn(x1, w_up, conv_w, conv_b, w_down, g, b):
    bsz, s, d = x1.shape
    tm = FFN_TM
    nu = w_up.shape[1]
    const = lambda shape: pl.BlockSpec(shape, lambda bi, i: (0, 0), pipeline_mode=pl.Buffered(1))
    return pl.pallas_call(
        _ffn_kernel,
        grid=(bsz, s // tm),
        in_specs=[
            pl.BlockSpec((None, tm, d), lambda bi, i: (bi, i, 0)),
            const((d, nu)),
            const((3, nu)),
            const((1, nu)),
            const((D_FF, d)),
            const((1, d)),
            const((1, d)),
        ],
        out_specs=pl.BlockSpec((None, tm, d), lambda bi, i: (bi, i, 0)),
        out_shape=jax.ShapeDtypeStruct((bsz, s, d), jnp.float32),
        scratch_shapes=[
            pltpu.VMEM((8, nu), jnp.float32),
            pltpu.VMEM((tm, D_FF), jnp.bfloat16),
        ],
        compiler_params=_params(("parallel", "arbitrary")),
        name="ffn",
    )(x1, w_up, conv_w, conv_b, w_down, g, b)


def _layer(x, w_in, b_forget, rel_bias, w_out, ln1_g, ln1_b, w_up, conv_w, conv_b, w_down, ln2_g, ln2_b):
    b, s, d = x.shape
    bf = jnp.bfloat16
    fcol = 3 * D_A
    bcol = fcol + N_HEADS_A
    wqk = jnp.concatenate([w_in[:, 0:2 * D_A], w_in[:, bcol:bcol + 2 * D_B]], axis=1).astype(bf)
    wvt = jnp.concatenate([w_in[:, 2 * D_A:fcol], w_in[:, bcol + 2 * D_B:]], axis=1).T.astype(bf)
    wf = jnp.pad(w_in[:, fcol:bcol], ((0, 0), (0, LANES - N_HEADS_A))).astype(bf)
    b_row = jnp.pad(b_forget, (0, LANES - N_HEADS_A)).reshape(1, LANES)

    qk, vt, f = _proj(x, wqk, wvt, wf)
    kaug = _gate(f, b_row)
    ya = _fox(qk, kaug, vt)
    yb = _band(qk, vt, _band_bias(rel_bias))
    x1 = _mix(ya.reshape(b * s, D_A), yb.reshape(b * s, D_B), x.reshape(b * s, d), w_out.astype(bf),
              ln1_g.reshape(1, d), ln1_b.reshape(1, d))
    return _ffn(x1.reshape(b, s, d), w_up.astype(bf), conv_w, conv_b.reshape(1, -1), w_down.astype(bf),
                ln2_g.reshape(1, d), ln2_b.reshape(1, d))


def kernel(x, w_in, b_forget, rel_bias, w_out, ln1_g, ln1_b, w_up, conv_w, conv_b, w_down, ln2_g, ln2_b):
    for l in range(DEPTH):
        x = _layer(x, w_in[l], b_forget[l], rel_bias[l], w_out[l], ln1_g[l], ln1_b[l], w_up[l],
                   conv_w[l], conv_b[l], w_down[l], ln2_g[l], ln2_b[l])
    return x
```

```python
import functools
import math

import jax
import jax.numpy as jnp
import numpy as np
from jax import lax
from jax.experimental import pallas as pl
from jax.experimental.pallas import tpu as pltpu

D_MODEL = 1024
HEAD_DIM = 64
N_HEADS_A = 8
N_HEADS_B = 8
D_A = N_HEADS_A * HEAD_DIM
D_B = N_HEADS_B * HEAD_DIM
CHUNK = 64
N_LEFT_CHUNKS = 8
REL_CLIP = 128
D_FF = 2816
LN_EPS = 1e-5
DEPTH = 1
DEEPNORM_ALPHA = (2.0 * DEPTH) ** 0.25

LANES = 128
PAIR = 2 * HEAD_DIM
N_PAIRS_A = D_A // PAIR
N_PAIRS_B = D_B // PAIR
ONES_ROWS = 16
NEG = -1e30
LOG2E = math.log2(math.e)
VMEM_LIMIT = 52 * 1024 * 1024

PROJ_TM = 512
GATE_TB = 512
FOX_T = 512
BAND_TQ = 512
BAND_G = 256
BAND_KW = BAND_G + N_LEFT_CHUNKS * CHUNK
MIX_TM = 512
FFN_TM = 256
FFN_CW = 256

_NT = (((1,), (1,)), ((), ()))


def _params(sem):
    return pltpu.CompilerParams(dimension_semantics=sem, vmem_limit_bytes=VMEM_LIMIT)


def _proj_kernel(x_ref, wqk_ref, wvt_ref, wf_ref, qk_ref, vt_ref, f_ref):
    xb = x_ref[...].astype(jnp.bfloat16)
    qk = jnp.dot(xb, wqk_ref[...], preferred_element_type=jnp.float32)
    scale = HEAD_DIM ** -0.5 * LOG2E
    col = lax.broadcasted_iota(jnp.int32, (1, qk.shape[1]), 1)
    is_q = (col < D_A) | ((col >= 2 * D_A) & (col < 2 * D_A + D_B))
    qk_ref[...] = (qk * jnp.where(is_q, scale, 1.0)).astype(jnp.bfloat16)
    vt = lax.dot_general(wvt_ref[...], xb, _NT, preferred_element_type=jnp.float32)
    vt_ref[...] = vt.astype(jnp.bfloat16)
    f_ref[...] = jnp.dot(xb, wf_ref[...], preferred_element_type=jnp.float32)


def _proj(x, wqk, wvt, wf):
    b, s, d = x.shape
    tm = PROJ_TM
    nqk = wqk.shape[1]
    nv = wvt.shape[0]
    return pl.pallas_call(
        _proj_kernel,
        grid=(b, s // tm),
        in_specs=[
            pl.BlockSpec((None, tm, d), lambda bi, i: (bi, i, 0)),
            pl.BlockSpec((d, nqk), lambda bi, i: (0, 0)),
            pl.BlockSpec((nv, d), lambda bi, i: (0, 0)),
            pl.BlockSpec((d, LANES), lambda bi, i: (0, 0)),
        ],
        out_specs=[
            pl.BlockSpec((None, tm, nqk), lambda bi, i: (bi, i, 0)),
            pl.BlockSpec((None, nv, tm), lambda bi, i: (bi, 0, i)),
            pl.BlockSpec((None, tm, LANES), lambda bi, i: (bi, i, 0)),
        ],
        out_shape=[
            jax.ShapeDtypeStruct((b, s, nqk), jnp.bfloat16),
            jax.ShapeDtypeStruct((b, nv, s), jnp.bfloat16),
            jax.ShapeDtypeStruct((b, s, LANES), jnp.float32),
        ],
        compiler_params=_params(("parallel", "parallel")),
        name="proj",
    )(x, wqk, wvt, wf)


def _split3(v):
    h1 = v.astype(jnp.bfloat16).astype(jnp.float32)
    r1 = v - h1
    h2 = r1.astype(jnp.bfloat16).astype(jnp.float32)
    h3 = r1 - h2
    return h1, h2, h3


def _gate_kernel(f_ref, b_ref, ka_ref, carry_ref):
    i = pl.program_id(1)

    @pl.when(i == 0)
    def _():
        carry_ref[...] = jnp.zeros_like(carry_ref)

    z = f_ref[...] + b_ref[...]
    logf = -(jnp.maximum(-z, 0.0) + jnp.log1p(jnp.exp(-jnp.abs(z))))
    tb = z.shape[0]
    r = lax.broadcasted_iota(jnp.int32, (tb, tb), 0)
    c = lax.broadcasted_iota(jnp.int32, (tb, tb), 1)
    tri = jnp.where(c <= r, 1.0, 0.0).astype(jnp.bfloat16)
    cum = carry_ref[0:1, :]
    for part in _split3(logf):
        cum = cum + jnp.dot(tri, part.astype(jnp.bfloat16), preferred_element_type=jnp.float32)
    carry_ref[...] = jnp.broadcast_to(cum[tb - 1:tb, :], carry_ref.shape)
    neg = cum * -LOG2E
    lane = lax.broadcasted_iota(jnp.int32, (tb, LANES), 1)
    for hp in range(N_PAIRS_A):
        out = jnp.zeros((tb, LANES), jnp.float32)
        for j in range(2):
            parts = _split3(neg[:, 2 * hp + j:2 * hp + j + 1])
            for t, p in enumerate(parts):
                out = jnp.where(lane == 3 * j + t, p, out)
        ka_ref[hp] = out.astype(jnp.bfloat16)


def _gate(f, b_row):
    b, s, _ = f.shape
    tb = GATE_TB
    return pl.pallas_call(
        _gate_kernel,
        grid=(b, s // tb),
        in_specs=[
            pl.BlockSpec((None, tb, LANES), lambda bi, i: (bi, i, 0)),
            pl.BlockSpec((1, LANES), lambda bi, i: (0, 0)),
        ],
        out_specs=pl.BlockSpec((None, N_PAIRS_A, tb, LANES), lambda bi, i: (bi, 0, i, 0)),
        out_shape=jax.ShapeDtypeStruct((b, N_PAIRS_A, s, LANES), jnp.bfloat16),
        scratch_shapes=[pltpu.VMEM((8, LANES), jnp.float32)],
        compiler_params=_params(("parallel", "arbitrary")),
        name="gate",
    )(f, b_row)


def _split_heads(q2):
    lane = lax.broadcasted_iota(jnp.int32, q2.shape, 1)
    qf = q2.astype(jnp.float32)
    even = jnp.where(lane < HEAD_DIM, qf, 0.0).astype(jnp.bfloat16)
    odd = jnp.where(lane < HEAD_DIM, 0.0, qf).astype(jnp.bfloat16)
    return even, odd


def _finalize(acc_e, acc_o):
    oe = acc_e[0:HEAD_DIM, :] / acc_e[PAIR:PAIR + 1, :]
    oo = acc_o[HEAD_DIM:PAIR, :] / acc_o[PAIR:PAIR + 1, :]
    return jnp.concatenate([oe, oo], axis=0).T


def _fox_kernel(q_ref, k_ref, ka_ref, vt_ref, o_ref, qe_ref, qo_ref, sa_ref, sb_ref, acc_e, acc_o, m_e, m_o):
    qi = pl.program_id(2)
    t = FOX_T
    q2 = q_ref[...]
    lane = lax.broadcasted_iota(jnp.int32, q2.shape, 1)
    qe_ref[:, 0:LANES], qo_ref[:, 0:LANES] = _split_heads(q2)
    qe_ref[:, LANES:] = jnp.where(lane < 3, 1.0, 0.0).astype(jnp.bfloat16)
    qo_ref[:, LANES:] = jnp.where((lane >= 3) & (lane < 6), 1.0, 0.0).astype(jnp.bfloat16)
    acc_e[...] = jnp.zeros_like(acc_e)
    acc_o[...] = jnp.zeros_like(acc_o)
    m_e[...] = jnp.full_like(m_e, NEG)
    m_o[...] = jnp.full_like(m_o, NEG)
    ones_rows = jnp.ones((ONES_ROWS, t), jnp.bfloat16)

    def scores(ki, s_ref):
        ks = pl.multiple_of(ki * t, t)
        kcat = jnp.concatenate([k_ref[pl.ds(ks, t), :], ka_ref[pl.ds(ks, t), :]], axis=1)
        for h, qx in enumerate((qe_ref, qo_ref)):
            s_ref[h] = lax.dot_general(kcat, qx[...], _NT, preferred_element_type=jnp.float32)

    def update(ki, s_ref, masked):
        ks = pl.multiple_of(ki * t, t)
        vt = jnp.concatenate([vt_ref[:, pl.ds(ks, t)], ones_rows], axis=0)
        for h, (acc, m) in enumerate(((acc_e, m_e), (acc_o, m_o))):
            st = s_ref[h]
            if masked:
                kr = lax.broadcasted_iota(jnp.int32, st.shape, 0)
                qc = lax.broadcasted_iota(jnp.int32, st.shape, 1)
                st = jnp.where(kr <= qc, st, NEG)
            m_old = m[...]
            m_new = jnp.maximum(m_old, jnp.max(st, axis=0, keepdims=True))
            p = jnp.exp2(st - m_new).astype(jnp.bfloat16)
            acc[...] = acc[...] * jnp.exp2(m_old - m_new) + jnp.dot(vt, p, preferred_element_type=jnp.float32)
            m[...] = m_new

    scores(0, sa_ref)

    def body(j, carry):
        scores(2 * j + 1, sb_ref)
        update(2 * j, sa_ref, False)
        scores(2 * j + 2, sa_ref)
        update(2 * j + 1, sb_ref, False)
        return carry

    lax.fori_loop(0, qi // 2, body, 0)

    @pl.when(qi % 2 == 0)
    def _():
        update(qi, sa_ref, True)

    @pl.when(qi % 2 == 1)
    def _():
        scores(qi, sb_ref)
        update(qi - 1, sa_ref, False)
        update(qi, sb_ref, True)

    o_ref[...] = _finalize(acc_e[...], acc_o[...]).astype(o_ref.dtype)


def _fox(qk, kaug, vt):
    b, s, _ = qk.shape
    t = FOX_T
    kcol0 = D_A // LANES
    return pl.pallas_call(
        _fox_kernel,
        grid=(b, N_PAIRS_A, s // t),
        in_specs=[
            pl.BlockSpec((None, t, LANES), lambda bi, hp, qi: (bi, qi, hp)),
            pl.BlockSpec((None, s, LANES), lambda bi, hp, qi: (bi, 0, kcol0 + hp)),
            pl.BlockSpec((None, None, s, LANES), lambda bi, hp, qi: (bi, hp, 0, 0)),
            pl.BlockSpec((None, LANES, s), lambda bi, hp, qi: (bi, hp, 0)),
        ],
        out_specs=pl.BlockSpec((None, t, LANES), lambda bi, hp, qi: (bi, qi, hp)),
        out_shape=jax.ShapeDtypeStruct((b, s, D_A), jnp.bfloat16),
        scratch_shapes=[
            pltpu.VMEM((t, 2 * LANES), jnp.bfloat16),
            pltpu.VMEM((t, 2 * LANES), jnp.bfloat16),
            pltpu.VMEM((2, t, t), jnp.float32),
            pltpu.VMEM((2, t, t), jnp.float32),
            pltpu.VMEM((PAIR + ONES_ROWS, t), jnp.float32),
            pltpu.VMEM((PAIR + ONES_ROWS, t), jnp.float32),
            pltpu.VMEM((1, t), jnp.float32),
            pltpu.VMEM((1, t), jnp.float32),
        ],
        compiler_params=_params(("parallel", "parallel", "arbitrary")),
        name="fox",
    )(qk, qk, kaug, vt)


def _band_kernel(q_ref, kp_ref, kc_ref, vtp_ref, vtc_ref, bias_ref, o_ref, kwin, vtwin, s_scr, p_scr):
    i = pl.program_id(2)
    tq = BAND_TQ
    n_groups = tq // BAND_G
    lane = lax.broadcasted_iota(jnp.int32, (tq, LANES), 1)
    kwin[0:tq, 0:LANES] = kp_ref[...]
    kwin[tq:2 * tq, 0:LANES] = kc_ref[...]
    pen = jnp.where(i == 0, NEG, 0.0)
    kwin[0:tq, LANES:] = jnp.where(lane == 0, pen, 0.0).astype(jnp.bfloat16)
    kwin[tq:2 * tq, LANES:] = jnp.zeros((tq, LANES), jnp.bfloat16)
    vtwin[0:PAIR, 0:tq] = vtp_ref[...]
    vtwin[0:PAIR, tq:2 * tq] = vtc_ref[...]
    vtwin[PAIR:, :] = jnp.ones((ONES_ROWS, 2 * tq), jnp.bfloat16)
    lane_g = lax.broadcasted_iota(jnp.int32, (BAND_G, LANES), 1)
    ones_col = jnp.where(lane_g == 0, 1.0, 0.0).astype(jnp.bfloat16)
    for g in range(n_groups):
        q2 = q_ref[g * BAND_G:(g + 1) * BAND_G, :]
        kslab = kwin[g * BAND_G:g * BAND_G + BAND_KW, :]
        for h, qh in enumerate(_split_heads(q2)):
            qcat = jnp.concatenate([qh, ones_col], axis=1)
            s_scr[2 * g + h] = lax.dot_general(kslab, qcat, _NT, preferred_element_type=jnp.float32)
    for u in range(2 * n_groups):
        st = s_scr[u] + bias_ref[u % 2]
        m = jnp.max(st, axis=0, keepdims=True)
        p_scr[u] = jnp.exp2(st - m).astype(jnp.bfloat16)
    for g in range(n_groups):
        vslab = vtwin[:, g * BAND_G:g * BAND_G + BAND_KW]
        accs = [jnp.dot(vslab, p_scr[2 * g + h], preferred_element_type=jnp.float32) for h in range(2)]
        o_ref[g * BAND_G:(g + 1) * BAND_G, :] = _finalize(accs[0], accs[1]).astype(o_ref.dtype)


def _band(qk, vt, bias_t):
    b, s, _ = qk.shape
    tq = BAND_TQ
    qcol0 = 2 * D_A // LANES
    kcol0 = (2 * D_A + D_B) // LANES
    vrow0 = D_A // PAIR
    prev = lambda i: jnp.maximum(i - 1, 0)
    return pl.pallas_call(
        _band_kernel,
        grid=(b, N_PAIRS_B, s // tq),
        in_specs=[
            pl.BlockSpec((None, tq, LANES), lambda bi, hp, i: (bi, i, qcol0 + hp)),
            pl.BlockSpec((None, tq, LANES), lambda bi, hp, i: (bi, prev(i), kcol0 + hp)),
            pl.BlockSpec((None, tq, LANES), lambda bi, hp, i: (bi, i, kcol0 + hp)),
            pl.BlockSpec((None, PAIR, tq), lambda bi, hp, i: (bi, vrow0 + hp, prev(i))),
            pl.BlockSpec((None, PAIR, tq), lambda bi, hp, i: (bi, vrow0 + hp, i)),
            pl.BlockSpec((2, BAND_KW, BAND_G), lambda bi, hp, i: (hp, 0, 0)),
        ],
        out_specs=pl.BlockSpec((None, tq, LANES), lambda bi, hp, i: (bi, i, hp)),
        out_shape=jax.ShapeDtypeStruct((b, s, D_B), jnp.bfloat16),
        scratch_shapes=[
            pltpu.VMEM((2 * tq, 2 * LANES), jnp.bfloat16),
            pltpu.VMEM((PAIR + ONES_ROWS, 2 * tq), jnp.bfloat16),
            pltpu.VMEM((2 * tq // BAND_G, BAND_KW, BAND_G), jnp.float32),
            pltpu.VMEM((2 * tq // BAND_G, BAND_KW, BAND_G), jnp.bfloat16),
        ],
        compiler_params=_params(("parallel", "parallel", "arbitrary")),
        name="band",
    )(qk, qk, qk, vt, vt, bias_t)


BIAS_W = 1024


def _bias_kernel(g_ref, o_ref):
    row = jnp.broadcast_to(g_ref[...], (BAND_KW, BIAS_W))
    toe = pltpu.roll(row, 0, 1, stride=1, stride_axis=0)[:, 0:BAND_G]
    kc = lax.broadcasted_iota(jnp.int32, (BAND_KW, BAND_G), 0) // CHUNK
    qc = lax.broadcasted_iota(jnp.int32, (BAND_KW, BAND_G), 1) // CHUNK
    inband = (kc >= qc) & (kc <= qc + N_LEFT_CHUNKS)
    o_ref[...] = jnp.where(inband, toe * LOG2E, NEG)


def _band_bias(rel_bias):
    h = rel_bias.shape[0]
    j = np.arange(BIAS_W)
    off = np.where(j < BAND_G, j, j - BIAS_W)
    idx = np.clip(N_LEFT_CHUNKS * CHUNK + off, -REL_CLIP, REL_CLIP) + REL_CLIP
    g_ext = rel_bias.astype(jnp.float32)[:, idx].reshape(h, 1, BIAS_W)
    return pl.pallas_call(
        _bias_kernel,
        grid=(h,),
        in_specs=[pl.BlockSpec((None, 1, BIAS_W), lambda i: (i, 0, 0))],
        out_specs=pl.BlockSpec((None, BAND_KW, BAND_G), lambda i: (i, 0, 0)),
        out_shape=jax.ShapeDtypeStruct((h, BAND_KW, BAND_G), jnp.float32),
        compiler_params=_params(("parallel",)),
        name="bias",
    )(g_ext)


def _layer_norm(z, g, b):
    mu = jnp.mean(z, axis=-1, keepdims=True)
    zc = z - mu
    var = jnp.mean(zc * zc, axis=-1, keepdims=True)
    return zc * lax.rsqrt(var + LN_EPS) * g + b


def _mix_kernel(ya_ref, yb_ref, x_ref, w_ref, g_ref, b_ref, o_ref):
    y = jnp.concatenate([ya_ref[...], yb_ref[...]], axis=1)
    mix = jnp.dot(y, w_ref[...], preferred_element_type=jnp.float32)
    o_ref[...] = _layer_norm(DEEPNORM_ALPHA * x_ref[...] + mix, g_ref[...], b_ref[...])


def _mix(ya, yb, x, w_out, g, b):
    m, d = x.shape
    tm = MIX_TM
    row = pl.BlockSpec((1, d), lambda i: (0, 0))
    return pl.pallas_call(
        _mix_kernel,
        grid=(m // tm,),
        in_specs=[
            pl.BlockSpec((tm, ya.shape[1]), lambda i: (i, 0)),
            pl.BlockSpec((tm, yb.shape[1]), lambda i: (i, 0)),
            pl.BlockSpec((tm, d), lambda i: (i, 0)),
            pl.BlockSpec((d, d), lambda i: (0, 0)),
            row, row,
        ],
        out_specs=pl.BlockSpec((tm, d), lambda i: (i, 0)),
        out_shape=jax.ShapeDtypeStruct((m, d), jnp.float32),
        compiler_params=_params(("parallel",)),
        name="mix",
    )(ya, yb, x, w_out, g, b)


def _gelu_tanh(x):
    return 0.5 * x * (1.0 + jnp.tanh(math.sqrt(2.0 / math.pi) * (x + 0.044715 * (x * x * x))))


def _causal_conv(u, carry, w, bias):
    tm = u.shape[0]
    r8 = lax.broadcasted_iota(jnp.int32, (8, u.shape[1]), 0)
    outs = w[2:3, :] * u + bias
    for shift, j in ((1, 1), (2, 0)):
        ur = pltpu.roll(u, shift, 0)
        cr = pltpu.roll(carry, shift, 0)
        head = jnp.where(r8 < shift, cr, ur[0:8, :])
        us = jnp.concatenate([head, ur[8:tm, :]], axis=0)
        outs = outs + w[j:j + 1, :] * us
    return outs


def _ffn_kernel(x_ref, wup_ref, cw_ref, cb_ref, wdn_ref, g_ref, b_ref, o_ref, carry_ref, h_ref):
    i = pl.program_id(1)

    @pl.when(i == 0)
    def _():
        carry_ref[...] = jnp.zeros_like(carry_ref)

    x = x_ref[...]
    xb = x.astype(jnp.bfloat16)
    tm = x.shape[0]
    for c in range(D_FF // FFN_CW):
        halves = []
        for off in (c * FFN_CW, D_FF + c * FFN_CW):
            sl = slice(off, off + FFN_CW)
            u = jnp.dot(xb, wup_ref[:, sl], preferred_element_type=jnp.float32)
            halves.append(_causal_conv(u, carry_ref[:, sl], cw_ref[:, sl], cb_ref[:, sl]))
            carry_ref[:, sl] = u[tm - 8:tm, :]
        h_ref[:, c * FFN_CW:(c + 1) * FFN_CW] = (halves[0] * _gelu_tanh(halves[1])).astype(jnp.bfloat16)
    ffn = jnp.dot(h_ref[...], wdn_ref[...], preferred_element_type=jnp.float32)
    o_ref[...] = _layer_norm(DEEPNORM_ALPHA * x + ffn, g_ref[...], b_ref[...])


def _ffn(x1, w_up, conv_w, conv_b, w_down, g, b):
    bsz, s, d = x1.shape
    tm = FFN_TM
    nu = w_up.shape[1]
    const = lambda shape: pl.BlockSpec(shape, lambda bi, i: (0, 0), pipeline_mode=pl.Buffered(1))
    return pl.pallas_call(
        _ffn_kernel,
        grid=(bsz, s // tm),
        in_specs=[
            pl.BlockSpec((None, tm, d), lambda bi, i: (bi, i, 0)),
            const((d, nu)),
            const((3, nu)),
            const((1, nu)),
            const((D_FF, d)),
            const((1, d)),
            const((1, d)),
        ],
        out_specs=pl.BlockSpec((None, tm, d), lambda bi, i: (bi, i, 0)),
        out_shape=jax.ShapeDtypeStruct((bsz, s, d), jnp.float32),
        scratch_shapes=[
            pltpu.VMEM((8, nu), jnp.float32),
            pltpu.VMEM((tm, D_FF), jnp.bfloat16),
        ],
        compiler_params=_params(("parallel", "arbitrary")),
        name="ffn",
    )(x1, w_up, conv_w, conv_b, w_down, g, b)


def _layer(x, w_in, b_forget, rel_bias, w_out, ln1_g, ln1_b, w_up, conv_w, conv_b, w_down, ln2_g, ln2_b):
    b, s, d = x.shape
    bf = jnp.bfloat16
    fcol = 3 * D_A
    bcol = fcol + N_HEADS_A
    wqk = jnp.concatenate([w_in[:, 0:2 * D_A], w_in[:, bcol:bcol + 2 * D_B]], axis=1).astype(bf)
    wvt = jnp.concatenate([w_in[:, 2 * D_A:fcol], w_in[:, bcol + 2 * D_B:]], axis=1).T.astype(bf)
    wf = jnp.pad(w_in[:, fcol:bcol], ((0, 0), (0, LANES - N_HEADS_A))).astype(bf)
    b_row = jnp.pad(b_forget, (0, LANES - N_HEADS_A)).reshape(1, LANES)

    qk, vt, f = _proj(x, wqk, wvt, wf)
    kaug = _gate(f, b_row)
    ya = _fox(qk, kaug, vt)
    yb = _band(qk, vt, _band_bias(rel_bias))
    x1 = _mix(ya.reshape(b * s, D_A), yb.reshape(b * s, D_B), x.reshape(b * s, d), w_out.astype(bf),
              ln1_g.reshape(1, d), ln1_b.reshape(1, d))
    return _ffn(x1.reshape(b, s, d), w_up.astype(bf), conv_w, conv_b.reshape(1, -1), w_down.astype(bf),
                ln2_g.reshape(1, d), ln2_b.reshape(1, d))


def kernel(x, w_in, b_forget, rel_bias, w_out, ln1_g, ln1_b, w_up, conv_w, conv_b, w_down, ln2_g, ln2_b):
    for l in range(DEPTH):
        x = _layer(x, w_in[l], b_forget[l], rel_bias[l], w_out[l], ln1_g[l], ln1_b[l], w_up[l],
                   conv_w[l], conv_b[l], w_down[l], ln2_g[l], ln2_b[l])
    return x
```

```python
import math

import jax
import jax.numpy as jnp
import numpy as np
from jax import lax
from jax.experimental import pallas as pl
from jax.experimental.pallas import tpu as pltpu

D_MODEL = 1024
HEAD_DIM = 64
N_HEADS_A = 8
N_HEADS_B = 8
D_A = N_HEADS_A * HEAD_DIM
D_B = N_HEADS_B * HEAD_DIM
CHUNK = 64
N_LEFT_CHUNKS = 8
REL_CLIP = 128
D_FF = 2816
LN_EPS = 1e-5
DEPTH = 1
DEEPNORM_ALPHA = (2.0 * DEPTH) ** 0.25

LANES = 128
PAIR = 2 * HEAD_DIM
N_PAIRS_A = D_A // PAIR
N_PAIRS_B = D_B // PAIR
ONES_ROWS = 16
NEG = -1e30
LOG2E = math.log2(math.e)
VMEM_LIMIT = 52 * 1024 * 1024

PROJ_TM = 512
GATE_TB = 512
FOX_TQ = 1024
FOX_TK = 512
BAND_TQ = 512
BAND_G = 256
BAND_KW = BAND_G + N_LEFT_CHUNKS * CHUNK
FFN_TM = 512
FFN_CW = 256

_NT = (((1,), (1,)), ((), ()))


def _params(sem, flags=None):
    return pltpu.CompilerParams(dimension_semantics=sem, vmem_limit_bytes=VMEM_LIMIT, flags=flags)


def _proj_kernel(x_ref, wqk_ref, wvt_ref, wf_ref, qk_ref, vt_ref, f_ref):
    xb = x_ref[...].astype(jnp.bfloat16)
    qk = jnp.dot(xb, wqk_ref[...], preferred_element_type=jnp.float32)
    scale = HEAD_DIM ** -0.5 * LOG2E
    col = lax.broadcasted_iota(jnp.int32, (1, qk.shape[1]), 1)
    is_q = (col < D_A) | ((col >= 2 * D_A) & (col < 2 * D_A + D_B))
    qk_ref[...] = (qk * jnp.where(is_q, scale, 1.0)).astype(jnp.bfloat16)
    vt = lax.dot_general(wvt_ref[...], xb, _NT, preferred_element_type=jnp.float32)
    vt_ref[...] = vt.astype(jnp.bfloat16)
    f_ref[...] = jnp.dot(xb, wf_ref[...], preferred_element_type=jnp.float32)


def _proj(x, wqk, wvt, wf):
    b, s, d = x.shape
    tm = PROJ_TM
    nqk = wqk.shape[1]
    nv = wvt.shape[0]
    return pl.pallas_call(
        _proj_kernel,
        grid=(b, s // tm),
        in_specs=[
            pl.BlockSpec((None, tm, d), lambda bi, i: (bi, i, 0)),
            pl.BlockSpec((d, nqk), lambda bi, i: (0, 0)),
            pl.BlockSpec((nv, d), lambda bi, i: (0, 0)),
            pl.BlockSpec((d, LANES), lambda bi, i: (0, 0)),
        ],
        out_specs=[
            pl.BlockSpec((None, tm, nqk), lambda bi, i: (bi, i, 0)),
            pl.BlockSpec((None, nv, tm), lambda bi, i: (bi, 0, i)),
            pl.BlockSpec((None, tm, LANES), lambda bi, i: (bi, i, 0)),
        ],
        out_shape=[
            jax.ShapeDtypeStruct((b, s, nqk), jnp.bfloat16),
            jax.ShapeDtypeStruct((b, nv, s), jnp.bfloat16),
            jax.ShapeDtypeStruct((b, s, LANES), jnp.float32),
        ],
        compiler_params=_params(("parallel", "parallel")),
        name="proj",
    )(x, wqk, wvt, wf)


def _split3(v):
    h1 = v.astype(jnp.bfloat16).astype(jnp.float32)
    r1 = v - h1
    h2 = r1.astype(jnp.bfloat16).astype(jnp.float32)
    h3 = r1 - h2
    return h1, h2, h3


def _gate_kernel(f_ref, b_ref, tri_ref, place_ref, ka_ref, carry_ref):
    i = pl.program_id(1)

    @pl.when(i == 0)
    def _():
        carry_ref[...] = jnp.zeros_like(carry_ref)

    z = f_ref[...] + b_ref[...]
    logf = -(jnp.maximum(-z, 0.0) + jnp.log1p(jnp.exp(-jnp.abs(z))))
    tb = z.shape[0]
    parts = jnp.concatenate(_split3(logf), axis=1).astype(jnp.bfloat16)
    sums = jnp.dot(tri_ref[...], parts, preferred_element_type=jnp.float32)
    cum = carry_ref[0:1, :] + (sums[:, 0:LANES] + sums[:, LANES:2 * LANES] + sums[:, 2 * LANES:])
    carry_ref[...] = jnp.broadcast_to(cum[tb - 1:tb, :], carry_ref.shape)
    neg = cum * -LOG2E
    nparts = jnp.concatenate(_split3(neg), axis=1).astype(jnp.bfloat16)
    for hp in range(N_PAIRS_A):
        ka_ref[hp] = jnp.dot(nparts, place_ref[hp], preferred_element_type=jnp.float32).astype(jnp.bfloat16)


def _gate_constants():
    tri = np.tril(np.ones((GATE_TB, GATE_TB), np.float32))
    place = np.zeros((N_PAIRS_A, 3 * LANES, LANES), np.float32)
    for hp in range(N_PAIRS_A):
        for j in range(2):
            for t in range(3):
                place[hp, t * LANES + 2 * hp + j, 3 * j + t] = 1.0
    return jnp.asarray(tri, jnp.bfloat16), jnp.asarray(place, jnp.bfloat16)


def _gate(f, b_row):
    b, s, _ = f.shape
    tb = GATE_TB
    tri, place = _gate_constants()
    return pl.pallas_call(
        _gate_kernel,
        grid=(b, s // tb),
        in_specs=[
            pl.BlockSpec((None, tb, LANES), lambda bi, i: (bi, i, 0)),
            pl.BlockSpec((1, LANES), lambda bi, i: (0, 0)),
            pl.BlockSpec((tb, tb), lambda bi, i: (0, 0)),
            pl.BlockSpec((N_PAIRS_A, 3 * LANES, LANES), lambda bi, i: (0, 0, 0)),
        ],
        out_specs=pl.BlockSpec((None, N_PAIRS_A, tb, LANES), lambda bi, i: (bi, 0, i, 0)),
        out_shape=jax.ShapeDtypeStruct((b, N_PAIRS_A, s, LANES), jnp.bfloat16),
        scratch_shapes=[pltpu.VMEM((8, LANES), jnp.float32)],
        compiler_params=_params(("parallel", "arbitrary")),
        name="gate",
    )(f, b_row, tri, place)


def _split_heads(q2):
    lane = lax.broadcasted_iota(jnp.int32, q2.shape, 1)
    qf = q2.astype(jnp.float32)
    even = jnp.where(lane < HEAD_DIM, qf, 0.0).astype(jnp.bfloat16)
    odd = jnp.where(lane < HEAD_DIM, 0.0, qf).astype(jnp.bfloat16)
    return even, odd


def _finalize(acc_e, acc_o):
    oe = acc_e[0:HEAD_DIM, :] / acc_e[HEAD_DIM:HEAD_DIM + 1, :]
    oo = acc_o[0:HEAD_DIM, :] / acc_o[HEAD_DIM:HEAD_DIM + 1, :]
    return jnp.concatenate([oe, oo], axis=0).T


def _vt_with_ones(vt_pair, h):
    ones_rows = jnp.ones((ONES_ROWS, vt_pair.shape[1]), jnp.bfloat16)
    return jnp.concatenate([vt_pair[h * HEAD_DIM:(h + 1) * HEAD_DIM, :], ones_rows], axis=0)


def _fox_kernel(q_ref, k_ref, ka_ref, vt_ref, o_ref, qe_ref, qo_ref, sa_ref, sb_ref, ma_ref, mb_ref,
                acc_e, acc_o, m_e, m_o):
    qi = pl.program_id(2)
    tq, tk = FOX_TQ, FOX_TK
    half = tq // 2
    q2 = q_ref[...]
    lane = lax.broadcasted_iota(jnp.int32, q2.shape, 1)
    qe_ref[:, 0:LANES], qo_ref[:, 0:LANES] = _split_heads(q2)
    qe_ref[:, LANES:] = jnp.where(lane < 3, 1.0, 0.0).astype(jnp.bfloat16)
    qo_ref[:, LANES:] = jnp.where((lane >= 3) & (lane < 6), 1.0, 0.0).astype(jnp.bfloat16)
    acc_e[...] = jnp.zeros_like(acc_e)
    acc_o[...] = jnp.zeros_like(acc_o)
    m_e[...] = jnp.full_like(m_e, NEG)
    m_o[...] = jnp.full_like(m_o, NEG)

    def scores(kb, s_ref, mx_ref, c0=0, ncol=tq):
        ks = pl.multiple_of(kb * tk, tk)
        kcat = jnp.concatenate([k_ref[pl.ds(ks, tk), :], ka_ref[pl.ds(ks, tk), :]], axis=1)
        for h, qx in enumerate((qe_ref, qo_ref)):
            st = lax.dot_general(kcat, qx[c0:c0 + ncol, :], _NT, preferred_element_type=jnp.float32)
            s_ref[h, :, c0:c0 + ncol] = st
            mx_ref[h, :, c0:c0 + ncol] = jnp.max(st, axis=0, keepdims=True)

    def update(kb, s_ref, mx_ref, masked, c0=0, ncol=tq):
        ks = pl.multiple_of(kb * tk, tk)
        vt_pair = vt_ref[:, pl.ds(ks, tk)]
        for h, (acc, m) in enumerate(((acc_e, m_e), (acc_o, m_o))):
            st = s_ref[h, :, c0:c0 + ncol]
            if masked:
                kr = lax.broadcasted_iota(jnp.int32, st.shape, 0)
                qc = lax.broadcasted_iota(jnp.int32, st.shape, 1)
                st = jnp.where(kr <= qc, st, NEG)
                blk_max = jnp.max(st, axis=0, keepdims=True)
            else:
                blk_max = mx_ref[h, :, c0:c0 + ncol]
            m_old = m[:, c0:c0 + ncol]
            m_new = jnp.maximum(m_old, blk_max)
            p = jnp.exp2(st - m_new).astype(jnp.bfloat16)
            pv = jnp.dot(_vt_with_ones(vt_pair, h), p, preferred_element_type=jnp.float32)
            acc[:, c0:c0 + ncol] = acc[:, c0:c0 + ncol] * jnp.exp2(m_old - m_new) + pv
            m[:, c0:c0 + ncol] = m_new

    kb0 = 2 * qi
    scores(0, sa_ref, ma_ref)

    def body(j, carry):
        scores(2 * j + 1, sb_ref, mb_ref)
        update(2 * j, sa_ref, ma_ref, False)
        scores(2 * j + 2, sa_ref, ma_ref)
        update(2 * j + 1, sb_ref, mb_ref, False)
        return carry

    lax.fori_loop(0, qi, body, 0)
    scores(kb0 + 1, sb_ref, mb_ref, half, half)
    update(kb0, sa_ref, ma_ref, True, 0, half)
    update(kb0, sa_ref, ma_ref, False, half, half)
    update(kb0 + 1, sb_ref, mb_ref, True, half, half)
    o_ref[...] = _finalize(acc_e[...], acc_o[...]).astype(o_ref.dtype)


def _fox(qk, kaug, vt):
    b, s, _ = qk.shape
    tq, tk = FOX_TQ, FOX_TK
    kcol0 = D_A // LANES
    return pl.pallas_call(
        _fox_kernel,
        grid=(b, N_PAIRS_A, s // tq),
        in_specs=[
            pl.BlockSpec((None, tq, LANES), lambda bi, hp, qi: (bi, qi, hp)),
            pl.BlockSpec((None, s, LANES), lambda bi, hp, qi: (bi, 0, kcol0 + hp)),
            pl.BlockSpec((None, None, s, LANES), lambda bi, hp, qi: (bi, hp, 0, 0)),
            pl.BlockSpec((None, LANES, s), lambda bi, hp, qi: (bi, hp, 0)),
        ],
        out_specs=pl.BlockSpec((None, tq, LANES), lambda bi, hp, qi: (bi, qi, hp)),
        out_shape=jax.ShapeDtypeStruct((b, s, D_A), jnp.bfloat16),
        scratch_shapes=[
            pltpu.VMEM((tq, 2 * LANES), jnp.bfloat16),
            pltpu.VMEM((tq, 2 * LANES), jnp.bfloat16),
            pltpu.VMEM((2, tk, tq), jnp.float32),
            pltpu.VMEM((2, tk, tq), jnp.float32),
            pltpu.VMEM((2, 1, tq), jnp.float32),
            pltpu.VMEM((2, 1, tq), jnp.float32),
            pltpu.VMEM((HEAD_DIM + ONES_ROWS, tq), jnp.float32),
            pltpu.VMEM((HEAD_DIM + ONES_ROWS, tq), jnp.float32),
            pltpu.VMEM((1, tq), jnp.float32),
            pltpu.VMEM((1, tq), jnp.float32),
        ],
        compiler_params=_params(("parallel", "parallel", "arbitrary")),
        name="fox",
    )(qk, qk, kaug, vt)


def _band_kernel(q_ref, kp_ref, kc_ref, vtp_ref, vtc_ref, bias_ref, o_ref, kwin, vtwin, s_scr, p_scr):
    i = pl.program_id(2)
    tq = BAND_TQ
    n_groups = tq // BAND_G
    lane = lax.broadcasted_iota(jnp.int32, (tq, LANES), 1)
    kwin[0:tq, 0:LANES] = kp_ref[...]
    kwin[tq:2 * tq, 0:LANES] = kc_ref[...]
    pen = jnp.where(i == 0, NEG, 0.0)
    kwin[0:tq, LANES:] = jnp.where(lane == 0, pen, 0.0).astype(jnp.bfloat16)
    kwin[tq:2 * tq, LANES:] = jnp.zeros((tq, LANES), jnp.bfloat16)
    vtwin[:, 0:tq] = vtp_ref[...]
    vtwin[:, tq:2 * tq] = vtc_ref[...]
    lane_g = lax.broadcasted_iota(jnp.int32, (BAND_G, LANES), 1)
    ones_col = jnp.where(lane_g == 0, 1.0, 0.0).astype(jnp.bfloat16)
    for g in range(n_groups):
        q2 = q_ref[g * BAND_G:(g + 1) * BAND_G, :]
        kslab = kwin[g * BAND_G:g * BAND_G + BAND_KW, :]
        for h, qh in enumerate(_split_heads(q2)):
            qcat = jnp.concatenate([qh, ones_col], axis=1)
            s_scr[2 * g + h] = lax.dot_general(kslab, qcat, _NT, preferred_element_type=jnp.float32)
    for u in range(2 * n_groups):
        st = s_scr[u] + bias_ref[u % 2]
        m = jnp.max(st, axis=0, keepdims=True)
        p_scr[u] = jnp.exp2(st - m).astype(jnp.bfloat16)
    for g in range(n_groups):
        vslab = vtwin[:, g * BAND_G:g * BAND_G + BAND_KW]
        accs = [jnp.dot(_vt_with_ones(vslab, h), p_scr[2 * g + h], preferred_element_type=jnp.float32)
                for h in range(2)]
        o_ref[g * BAND_G:(g + 1) * BAND_G, :] = _finalize(accs[0], accs[1]).astype(o_ref.dtype)


def _band(qk, vt, bias_t):
    b, s, _ = qk.shape
    tq = BAND_TQ
    qcol0 = 2 * D_A // LANES
    kcol0 = (2 * D_A + D_B) // LANES
    vrow0 = D_A // PAIR
    prev = lambda i: jnp.maximum(i - 1, 0)
    return pl.pallas_call(
        _band_kernel,
        grid=(b, N_PAIRS_B, s // tq),
        in_specs=[
            pl.BlockSpec((None, tq, LANES), lambda bi, hp, i: (bi, i, qcol0 + hp)),
            pl.BlockSpec((None, tq, LANES), lambda bi, hp, i: (bi, prev(i), kcol0 + hp)),
            pl.BlockSpec((None, tq, LANES), lambda bi, hp, i: (bi, i, kcol0 + hp)),
            pl.BlockSpec((None, PAIR, tq), lambda bi, hp, i: (bi, vrow0 + hp, prev(i))),
            pl.BlockSpec((None, PAIR, tq), lambda bi, hp, i: (bi, vrow0 + hp, i)),
            pl.BlockSpec((2, BAND_KW, BAND_G), lambda bi, hp, i: (hp, 0, 0)),
        ],
        out_specs=pl.BlockSpec((None, tq, LANES), lambda bi, hp, i: (bi, i, hp)),
        out_shape=jax.ShapeDtypeStruct((b, s, D_B), jnp.bfloat16),
        scratch_shapes=[
            pltpu.VMEM((2 * tq, 2 * LANES), jnp.bfloat16),
            pltpu.VMEM((PAIR, 2 * tq), jnp.bfloat16),
            pltpu.VMEM((2 * tq // BAND_G, BAND_KW, BAND_G), jnp.float32),
            pltpu.VMEM((2 * tq // BAND_G, BAND_KW, BAND_G), jnp.bfloat16),
        ],
        compiler_params=_params(("parallel", "parallel", "arbitrary")),
        name="band",
    )(qk, qk, qk, vt, vt, bias_t)


BIAS_W = 1024


def _bias_kernel(g_ref, o_ref):
    row = jnp.broadcast_to(g_ref[...], (BAND_KW, BIAS_W))
    toe = pltpu.roll(row, 0, 1, stride=1, stride_axis=0)[:, 0:BAND_G]
    kc = lax.broadcasted_iota(jnp.int32, (BAND_KW, BAND_G), 0) // CHUNK
    qc = lax.broadcasted_iota(jnp.int32, (BAND_KW, BAND_G), 1) // CHUNK
    inband = (kc >= qc) & (kc <= qc + N_LEFT_CHUNKS)
    o_ref[...] = jnp.where(inband, toe * LOG2E, NEG)


def _band_bias(rel_bias):
    h = rel_bias.shape[0]
    j = np.arange(BIAS_W)
    off = np.where(j < BAND_G, j, j - BIAS_W)
    idx = np.clip(N_LEFT_CHUNKS * CHUNK + off, -REL_CLIP, REL_CLIP) + REL_CLIP
    g_ext = rel_bias.astype(jnp.float32)[:, idx].reshape(h, 1, BIAS_W)
    return pl.pallas_call(
        _bias_kernel,
        grid=(h,),
        in_specs=[pl.BlockSpec((None, 1, BIAS_W), lambda i: (i, 0, 0))],
        out_specs=pl.BlockSpec((None, BAND_KW, BAND_G), lambda i: (i, 0, 0)),
        out_shape=jax.ShapeDtypeStruct((h, BAND_KW, BAND_G), jnp.float32),
        compiler_params=_params(("parallel",)),
        name="bias",
    )(g_ext)


def _layer_norm(z, g, b):
    mu = jnp.mean(z, axis=-1, keepdims=True)
    zc = z - mu
    var = jnp.mean(zc * zc, axis=-1, keepdims=True)
    return zc * lax.rsqrt(var + LN_EPS) * g + b


def _gelu_tanh(x):
    return 0.5 * x * (1.0 + jnp.tanh(math.sqrt(2.0 / math.pi) * (x + 0.044715 * (x * x * x))))


def _causal_conv(u, carry, w, bias):
    tm = u.shape[0]
    r8 = lax.broadcasted_iota(jnp.int32, (8, u.shape[1]), 0)
    outs = w[2:3, :] * u + bias
    for shift, j in ((1, 1), (2, 0)):
        ur = pltpu.roll(u, shift, 0)
        cr = pltpu.roll(carry, shift, 0)
        head = jnp.where(r8 < shift, cr, ur[0:8, :])
        us = jnp.concatenate([head, ur[8:tm, :]], axis=0)
        outs = outs + w[j:j + 1, :] * us
    return outs


def _ffn_kernel(ya_ref, yb_ref, x_ref, wout_ref, g1_ref, b1_ref, wup_ref, cw_ref, cb_ref, wdn_ref, g2_ref, b2_ref,
                o_ref, carry_ref, h_ref):
    i = pl.program_id(1)

    @pl.when(i == 0)
    def _():
        carry_ref[...] = jnp.zeros_like(carry_ref)

    y = jnp.concatenate([ya_ref[...], yb_ref[...]], axis=1)
    mix = jnp.dot(y, wout_ref[...], preferred_element_type=jnp.float32)
    x1 = _layer_norm(DEEPNORM_ALPHA * x_ref[...] + mix, g1_ref[...], b1_ref[...])
    xb = x1.astype(jnp.bfloat16)
    tm = x1.shape[0]
    for c in range(D_FF // FFN_CW):
        halves = []
        for off in (c * FFN_CW, D_FF + c * FFN_CW):
            sl = slice(off, off + FFN_CW)
            u = jnp.dot(xb, wup_ref[:, sl], preferred_element_type=jnp.float32)
            halves.append(_causal_conv(u, carry_ref[:, sl], cw_ref[:, sl], cb_ref[:, sl]))
            carry_ref[:, sl] = u[tm - 8:tm, :]
        h_ref[:, c * FFN_CW:(c + 1) * FFN_CW] = (halves[0] * _gelu_tanh(halves[1])).astype(jnp.bfloat16)
    ffn = jnp.dot(h_ref[...], wdn_ref[...], preferred_element_type=jnp.float32)
    o_ref[...] = _layer_norm(DEEPNORM_ALPHA * x1 + ffn, g2_ref[...], b2_ref[...])


def _ffn(ya, yb, x, w_out, g1, b1, w_up, conv_w, conv_b, w_down, g2, b2):
    bsz, s, d = x.shape
    tm = FFN_TM
    nu = w_up.shape[1]
    const = lambda shape: pl.BlockSpec(shape, lambda bi, i: (0, 0), pipeline_mode=pl.Buffered(1))
    rows = lambda width: pl.BlockSpec((None, tm, width), lambda bi, i: (bi, i, 0))
    return pl.pallas_call(
        _ffn_kernel,
        grid=(bsz, s // tm),
        in_specs=[
            rows(ya.shape[2]), rows(yb.shape[2]), rows(d),
            const((d, d)), const((1, d)), const((1, d)),
            const((d, nu)), const((3, nu)), const((1, nu)), const((D_FF, d)), const((1, d)), const((1, d)),
        ],
        out_specs=rows(d),
        out_shape=jax.ShapeDtypeStruct((bsz, s, d), jnp.float32),
        scratch_shapes=[
            pltpu.VMEM((8, nu), jnp.float32),
            pltpu.VMEM((tm, D_FF), jnp.bfloat16),
        ],
        compiler_params=_params(("parallel", "arbitrary")),
        name="ffn",
    )(ya, yb, x, w_out, g1, b1, w_up, conv_w, conv_b, w_down, g2, b2)


def _layer(x, w_in, b_forget, rel_bias, w_out, ln1_g, ln1_b, w_up, conv_w, conv_b, w_down, ln2_g, ln2_b):
    b, s, d = x.shape
    bf = jnp.bfloat16
    fcol = 3 * D_A
    bcol = fcol + N_HEADS_A
    wqk = jnp.concatenate([w_in[:, 0:2 * D_A], w_in[:, bcol:bcol + 2 * D_B]], axis=1).astype(bf)
    wvt = jnp.concatenate([w_in[:, 2 * D_A:fcol], w_in[:, bcol + 2 * D_B:]], axis=1).T.astype(bf)
    wf = jnp.pad(w_in[:, fcol:bcol], ((0, 0), (0, LANES - N_HEADS_A))).astype(bf)
    b_row = jnp.pad(b_forget, (0, LANES - N_HEADS_A)).reshape(1, LANES)

    qk, vt, f = _proj(x, wqk, wvt, wf)
    kaug = _gate(f, b_row)
    ya = _fox(qk, kaug, vt)
    yb = _band(qk, vt, _band_bias(rel_bias))
    return _ffn(ya, yb, x, w_out.astype(bf), ln1_g.reshape(1, d), ln1_b.reshape(1, d),
                w_up.astype(bf), conv_w, conv_b.reshape(1, -1), w_down.astype(bf),
                ln2_g.reshape(1, d), ln2_b.reshape(1, d))


def kernel(x, w_in, b_forget, rel_bias, w_out, ln1_g, ln1_b, w_up, conv_w, conv_b, w_down, ln2_g, ln2_b):
    for l in range(DEPTH):
        x = _layer(x, w_in[l], b_forget[l], rel_bias[l], w_out[l], ln1_g[l], ln1_b[l], w_up[l],
                   conv_w[l], conv_b[l], w_down[l], ln2_g[l], ln2_b[l])
    return x
```

```python
import math

import jax
import jax.numpy as jnp
import numpy as np
from jax import lax
from jax.experimental import pallas as pl
from jax.experimental.pallas import tpu as pltpu

D_MODEL = 1024
HEAD_DIM = 64
N_HEADS_A = 8
N_HEADS_B = 8
D_A = N_HEADS_A * HEAD_DIM
D_B = N_HEADS_B * HEAD_DIM
CHUNK = 64
N_LEFT_CHUNKS = 8
REL_CLIP = 128
D_FF = 2816
LN_EPS = 1e-5
DEPTH = 1
DEEPNORM_ALPHA = (2.0 * DEPTH) ** 0.25

LANES = 128
PAIR = 2 * HEAD_DIM
N_PAIRS_A = D_A // PAIR
N_PAIRS_B = D_B // PAIR
ONES_ROWS = 16
NEG = -1e30
LOG2E = math.log2(math.e)
VMEM_LIMIT = 52 * 1024 * 1024

PROJ_TM = 512
GATE_TB = 512
FOX_TQ = 1024
FOX_TK = 512
FOX_STRIP = 256
BAND_HALO = N_LEFT_CHUNKS * CHUNK
BAND_TQ = 1024
BAND_G = 256
BAND_KW = BAND_G + BAND_HALO
FFN_TM = 512
FFN_CW = 256

_NT = (((1,), (1,)), ((), ()))


def _params(sem, flags=None):
    return pltpu.CompilerParams(dimension_semantics=sem, vmem_limit_bytes=VMEM_LIMIT, flags=flags)


def _proj_kernel(x_ref, wqk_ref, wvt_ref, wf_ref, qk_ref, vt_ref, f_ref):
    xb = x_ref[...].astype(jnp.bfloat16)
    qk = jnp.dot(xb, wqk_ref[...], preferred_element_type=jnp.float32)
    scale = HEAD_DIM ** -0.5 * LOG2E
    col = lax.broadcasted_iota(jnp.int32, (1, qk.shape[1]), 1)
    is_q = (col < D_A) | ((col >= 2 * D_A) & (col < 2 * D_A + D_B))
    qk_ref[...] = (qk * jnp.where(is_q, scale, 1.0)).astype(jnp.bfloat16)
    vt = lax.dot_general(wvt_ref[...], xb, _NT, preferred_element_type=jnp.float32)
    vt_ref[...] = vt.astype(jnp.bfloat16)
    f_ref[...] = jnp.dot(xb, wf_ref[...], preferred_element_type=jnp.float32)


def _proj(x, wqk, wvt, wf):
    b, s, d = x.shape
    tm = PROJ_TM
    nqk = wqk.shape[1]
    nv = wvt.shape[0]
    return pl.pallas_call(
        _proj_kernel,
        grid=(b, s // tm),
        in_specs=[
            pl.BlockSpec((None, tm, d), lambda bi, i: (bi, i, 0)),
            pl.BlockSpec((d, nqk), lambda bi, i: (0, 0)),
            pl.BlockSpec((nv, d), lambda bi, i: (0, 0)),
            pl.BlockSpec((d, LANES), lambda bi, i: (0, 0)),
        ],
        out_specs=[
            pl.BlockSpec((None, tm, nqk), lambda bi, i: (bi, i, 0)),
            pl.BlockSpec((None, nv, tm), lambda bi, i: (bi, 0, i)),
            pl.BlockSpec((None, tm, LANES), lambda bi, i: (bi, i, 0)),
        ],
        out_shape=[
            jax.ShapeDtypeStruct((b, s, nqk), jnp.bfloat16),
            jax.ShapeDtypeStruct((b, nv, s), jnp.bfloat16),
            jax.ShapeDtypeStruct((b, s, LANES), jnp.float32),
        ],
        compiler_params=_params(("parallel", "parallel")),
        name="proj",
    )(x, wqk, wvt, wf)


def _split3(v):
    h1 = v.astype(jnp.bfloat16).astype(jnp.float32)
    r1 = v - h1
    h2 = r1.astype(jnp.bfloat16).astype(jnp.float32)
    h3 = r1 - h2
    return h1, h2, h3


def _gate_kernel(f_ref, b_ref, tri_ref, place_ref, ka_ref, carry_ref):
    i = pl.program_id(1)

    @pl.when(i == 0)
    def _():
        carry_ref[...] = jnp.zeros_like(carry_ref)

    z = f_ref[...] + b_ref[...]
    logf = -(jnp.maximum(-z, 0.0) + jnp.log1p(jnp.exp(-jnp.abs(z))))
    tb = z.shape[0]
    parts = jnp.concatenate(_split3(logf), axis=1).astype(jnp.bfloat16)
    sums = jnp.dot(tri_ref[...], parts, preferred_element_type=jnp.float32)
    cum = carry_ref[0:1, :] + (sums[:, 0:LANES] + sums[:, LANES:2 * LANES] + sums[:, 2 * LANES:])
    carry_ref[...] = jnp.broadcast_to(cum[tb - 1:tb, :], carry_ref.shape)
    neg = cum * -LOG2E
    nparts = jnp.concatenate(_split3(neg), axis=1).astype(jnp.bfloat16)
    for hp in range(N_PAIRS_A):
        ka_ref[hp] = jnp.dot(nparts, place_ref[hp], preferred_element_type=jnp.float32).astype(jnp.bfloat16)


def _gate_constants():
    tri = np.tril(np.ones((GATE_TB, GATE_TB), np.float32))
    place = np.zeros((N_PAIRS_A, 3 * LANES, LANES), np.float32)
    for hp in range(N_PAIRS_A):
        for j in range(2):
            for t in range(3):
                place[hp, t * LANES + 2 * hp + j, 3 * j + t] = 1.0
    return jnp.asarray(tri, jnp.bfloat16), jnp.asarray(place, jnp.bfloat16)


def _gate(f, b_row):
    b, s, _ = f.shape
    tb = GATE_TB
    tri, place = _gate_constants()
    return pl.pallas_call(
        _gate_kernel,
        grid=(b, s // tb),
        in_specs=[
            pl.BlockSpec((None, tb, LANES), lambda bi, i: (bi, i, 0)),
            pl.BlockSpec((1, LANES), lambda bi, i: (0, 0)),
            pl.BlockSpec((tb, tb), lambda bi, i: (0, 0)),
            pl.BlockSpec((N_PAIRS_A, 3 * LANES, LANES), lambda bi, i: (0, 0, 0)),
        ],
        out_specs=pl.BlockSpec((None, N_PAIRS_A, tb, LANES), lambda bi, i: (bi, 0, i, 0)),
        out_shape=jax.ShapeDtypeStruct((b, N_PAIRS_A, s, LANES), jnp.bfloat16),
        scratch_shapes=[pltpu.VMEM((8, LANES), jnp.float32)],
        compiler_params=_params(("parallel", "arbitrary")),
        name="gate",
    )(f, b_row, tri, place)


def _split_heads(q2):
    lane = lax.broadcasted_iota(jnp.int32, q2.shape, 1)
    qf = q2.astype(jnp.float32)
    even = jnp.where(lane < HEAD_DIM, qf, 0.0).astype(jnp.bfloat16)
    odd = jnp.where(lane < HEAD_DIM, 0.0, qf).astype(jnp.bfloat16)
    return even, odd


def _finalize(acc_e, acc_o):
    oe = acc_e[0:HEAD_DIM, :] / acc_e[HEAD_DIM:HEAD_DIM + 1, :]
    oo = acc_o[0:HEAD_DIM, :] / acc_o[HEAD_DIM:HEAD_DIM + 1, :]
    return jnp.concatenate([oe, oo], axis=0).T


def _vt_with_ones(vt_pair, h):
    ones_rows = jnp.ones((ONES_ROWS, vt_pair.shape[1]), jnp.bfloat16)
    return jnp.concatenate([vt_pair[h * HEAD_DIM:(h + 1) * HEAD_DIM, :], ones_rows], axis=0)


def _fox_kernel(q_ref, k_ref, ka_ref, vt_ref, o_ref, qe_ref, qo_ref, sa_ref, sb_ref, ma_ref, mb_ref,
                acc_e, acc_o, m_e, m_o):
    qi = pl.program_id(2)
    tq, tk = FOX_TQ, FOX_TK
    half = tq // 2
    q2 = q_ref[...]
    lane = lax.broadcasted_iota(jnp.int32, q2.shape, 1)
    qe_ref[:, 0:LANES], qo_ref[:, 0:LANES] = _split_heads(q2)
    qe_ref[:, LANES:] = jnp.where(lane < 3, 1.0, 0.0).astype(jnp.bfloat16)
    qo_ref[:, LANES:] = jnp.where((lane >= 3) & (lane < 6), 1.0, 0.0).astype(jnp.bfloat16)
    acc_e[...] = jnp.zeros_like(acc_e)
    acc_o[...] = jnp.zeros_like(acc_o)
    m_e[...] = jnp.full_like(m_e, NEG)
    m_o[...] = jnp.full_like(m_o, NEG)

    q_refs = (qe_ref, qo_ref)
    stats = ((acc_e, m_e), (acc_o, m_o))

    def key_operands(kb):
        ks = pl.multiple_of(kb * tk, tk)
        kcat = jnp.concatenate([k_ref[pl.ds(ks, tk), :], ka_ref[pl.ds(ks, tk), :]], axis=1)
        return kcat, vt_ref[:, pl.ds(ks, tk)]

    def score_strip(kcat, h, c, s_ref, mx_ref):
        cs = slice(c, c + FOX_STRIP)
        st = lax.dot_general(kcat, q_refs[h][cs, :], _NT, preferred_element_type=jnp.float32)
        s_ref[h, :, cs] = st
        mx_ref[h, :, cs] = jnp.max(st, axis=0, keepdims=True)

    def update_strip(vt_h, h, c, s_ref, mx_ref, mask_c0=None):
        acc, m = stats[h]
        cs = slice(c, c + FOX_STRIP)
        st = s_ref[h, :, cs]
        if mask_c0 is None:
            blk_max = mx_ref[h, :, cs]
        else:
            kr = lax.broadcasted_iota(jnp.int32, st.shape, 0)
            qc = lax.broadcasted_iota(jnp.int32, st.shape, 1) + (c - mask_c0)
            st = jnp.where(kr <= qc, st, NEG)
            blk_max = jnp.max(st, axis=0, keepdims=True)
        m_old = m[:, cs]
        m_new = jnp.maximum(m_old, blk_max)
        p = jnp.exp2(st - m_new).astype(jnp.bfloat16)
        pv = jnp.dot(vt_h, p, preferred_element_type=jnp.float32)
        acc[:, cs] = acc[:, cs] * jnp.exp2(m_old - m_new) + pv
        m[:, cs] = m_new

    def scores(kb, s_ref, mx_ref, c0=0, ncol=tq):
        kcat, _ = key_operands(kb)
        for h in range(2):
            for c in range(c0, c0 + ncol, FOX_STRIP):
                score_strip(kcat, h, c, s_ref, mx_ref)

    def update(kb, s_ref, mx_ref, c0, ncol, mask_c0=None):
        _, vt_pair = key_operands(kb)
        for h in range(2):
            vt_h = _vt_with_ones(vt_pair, h)
            for c in range(c0, c0 + ncol, FOX_STRIP):
                update_strip(vt_h, h, c, s_ref, mx_ref, mask_c0)

    def step(kb_next, s_next, mx_next, kb_cur, s_cur, mx_cur):
        kcat, _ = key_operands(kb_next)
        _, vt_pair = key_operands(kb_cur)
        for h in range(2):
            vt_h = _vt_with_ones(vt_pair, h)
            for c in range(0, tq, FOX_STRIP):
                score_strip(kcat, h, c, s_next, mx_next)
                update_strip(vt_h, h, c, s_cur, mx_cur)

    kb0 = 2 * qi
    scores(0, sa_ref, ma_ref)

    def body(j, carry):
        step(2 * j + 1, sb_ref, mb_ref, 2 * j, sa_ref, ma_ref)
        step(2 * j + 2, sa_ref, ma_ref, 2 * j + 1, sb_ref, mb_ref)
        return carry

    lax.fori_loop(0, qi, body, 0)
    scores(kb0 + 1, sb_ref, mb_ref, half, half)
    update(kb0, sa_ref, ma_ref, 0, half, mask_c0=0)
    update(kb0, sa_ref, ma_ref, half, half)
    update(kb0 + 1, sb_ref, mb_ref, half, half, mask_c0=half)
    o_ref[...] = _finalize(acc_e[...], acc_o[...]).astype(o_ref.dtype)


def _fox(qk, kaug, vt):
    b, s, _ = qk.shape
    tq, tk = FOX_TQ, FOX_TK
    kcol0 = D_A // LANES
    return pl.pallas_call(
        _fox_kernel,
        grid=(b, N_PAIRS_A, s // tq),
        in_specs=[
            pl.BlockSpec((None, tq, LANES), lambda bi, hp, qi: (bi, qi, hp)),
            pl.BlockSpec((None, s, LANES), lambda bi, hp, qi: (bi, 0, kcol0 + hp)),
            pl.BlockSpec((None, None, s, LANES), lambda bi, hp, qi: (bi, hp, 0, 0)),
            pl.BlockSpec((None, LANES, s), lambda bi, hp, qi: (bi, hp, 0)),
        ],
        out_specs=pl.BlockSpec((None, tq, LANES), lambda bi, hp, qi: (bi, qi, hp)),
        out_shape=jax.ShapeDtypeStruct((b, s, D_A), jnp.bfloat16),
        scratch_shapes=[
            pltpu.VMEM((tq, 2 * LANES), jnp.bfloat16),
            pltpu.VMEM((tq, 2 * LANES), jnp.bfloat16),
            pltpu.VMEM((2, tk, tq), jnp.float32),
            pltpu.VMEM((2, tk, tq), jnp.float32),
            pltpu.VMEM((2, 1, tq), jnp.float32),
            pltpu.VMEM((2, 1, tq), jnp.float32),
            pltpu.VMEM((HEAD_DIM + ONES_ROWS, tq), jnp.float32),
            pltpu.VMEM((HEAD_DIM + ONES_ROWS, tq), jnp.float32),
            pltpu.VMEM((1, tq), jnp.float32),
            pltpu.VMEM((1, tq), jnp.float32),
        ],
        compiler_params=_params(("parallel", "parallel", "arbitrary")),
        name="fox",
    )(qk, qk, kaug, vt)


def _band_kernel(q_ref, kp_ref, kc_ref, vtp_ref, vtc_ref, bias_ref, o_ref, kwin, vtwin, s_scr, p_scr):
    i = pl.program_id(2)
    tq, halo = BAND_TQ, BAND_HALO
    n_units = 2 * (tq // BAND_G)
    lane = lax.broadcasted_iota(jnp.int32, (halo, LANES), 1)
    kwin[0:halo, 0:LANES] = kp_ref[...]
    kwin[halo:, 0:LANES] = kc_ref[...]
    pen = jnp.where(i == 0, NEG, 0.0)
    kwin[0:halo, LANES:] = jnp.where(lane == 0, pen, 0.0).astype(jnp.bfloat16)
    kwin[halo:, LANES:] = jnp.zeros((tq, LANES), jnp.bfloat16)
    vtwin[:, 0:halo] = vtp_ref[...]
    vtwin[:, halo:] = vtc_ref[...]
    lane_g = lax.broadcasted_iota(jnp.int32, (BAND_G, LANES), 1)
    ones_col = jnp.where(lane_g == 0, 1.0, 0.0).astype(jnp.bfloat16)

    def scores(u):
        g, h = divmod(u, 2)
        qh = _split_heads(q_ref[g * BAND_G:(g + 1) * BAND_G, :])[h]
        qcat = jnp.concatenate([qh, ones_col], axis=1)
        kslab = kwin[g * BAND_G:g * BAND_G + BAND_KW, :]
        s_scr[u] = lax.dot_general(kslab, qcat, _NT, preferred_element_type=jnp.float32)

    def softmax(u):
        st = s_scr[u] + bias_ref[u % 2]
        m = jnp.max(st, axis=0, keepdims=True)
        p_scr[u] = jnp.exp2(st - m).astype(jnp.bfloat16)

    accs = {}

    def pv(u):
        g, h = divmod(u, 2)
        vslab = vtwin[:, g * BAND_G:g * BAND_G + BAND_KW]
        accs[h] = jnp.dot(_vt_with_ones(vslab, h), p_scr[u], preferred_element_type=jnp.float32)
        if h == 1:
            o_ref[g * BAND_G:(g + 1) * BAND_G, :] = _finalize(accs[0], accs[1]).astype(o_ref.dtype)

    for t in range(n_units + 2):
        if t < n_units:
            scores(t)
        if 1 <= t <= n_units:
            softmax(t - 1)
        if t >= 2:
            pv(t - 2)


def _band(qk, vt, bias_t):
    b, s, _ = qk.shape
    tq, halo = BAND_TQ, BAND_HALO
    r = tq // halo
    qcol0 = 2 * D_A // LANES
    kcol0 = (2 * D_A + D_B) // LANES
    vrow0 = D_A // PAIR
    prev = lambda i: jnp.maximum(i * r - 1, 0)
    n_units = 2 * (tq // BAND_G)
    return pl.pallas_call(
        _band_kernel,
        grid=(b, N_PAIRS_B, s // tq),
        in_specs=[
            pl.BlockSpec((None, tq, LANES), lambda bi, hp, i: (bi, i, qcol0 + hp)),
            pl.BlockSpec((None, halo, LANES), lambda bi, hp, i: (bi, prev(i), kcol0 + hp)),
            pl.BlockSpec((None, tq, LANES), lambda bi, hp, i: (bi, i, kcol0 + hp)),
            pl.BlockSpec((None, PAIR, halo), lambda bi, hp, i: (bi, vrow0 + hp, prev(i))),
            pl.BlockSpec((None, PAIR, tq), lambda bi, hp, i: (bi, vrow0 + hp, i)),
            pl.BlockSpec((2, BAND_KW, BAND_G), lambda bi, hp, i: (hp, 0, 0)),
        ],
        out_specs=pl.BlockSpec((None, tq, LANES), lambda bi, hp, i: (bi, i, hp)),
        out_shape=jax.ShapeDtypeStruct((b, s, D_B), jnp.bfloat16),
        scratch_shapes=[
            pltpu.VMEM((halo + tq, 2 * LANES), jnp.bfloat16),
            pltpu.VMEM((PAIR, halo + tq), jnp.bfloat16),
            pltpu.VMEM((n_units, BAND_KW, BAND_G), jnp.float32),
            pltpu.VMEM((n_units, BAND_KW, BAND_G), jnp.bfloat16),
        ],
        compiler_params=_params(("parallel", "parallel", "arbitrary")),
        name="band",
    )(qk, qk, qk, vt, vt, bias_t)


BIAS_W = 1024


def _bias_kernel(g_ref, o_ref):
    row = jnp.broadcast_to(g_ref[...], (BAND_KW, BIAS_W))
    toe = pltpu.roll(row, 0, 1, stride=1, stride_axis=0)[:, 0:BAND_G]
    kc = lax.broadcasted_iota(jnp.int32, (BAND_KW, BAND_G), 0) // CHUNK
    qc = lax.broadcasted_iota(jnp.int32, (BAND_KW, BAND_G), 1) // CHUNK
    inband = (kc >= qc) & (kc <= qc + N_LEFT_CHUNKS)
    o_ref[...] = jnp.where(inband, toe * LOG2E, NEG)


def _band_bias(rel_bias):
    h = rel_bias.shape[0]
    j = np.arange(BIAS_W)
    off = np.where(j < BAND_G, j, j - BIAS_W)
    idx = np.clip(N_LEFT_CHUNKS * CHUNK + off, -REL_CLIP, REL_CLIP) + REL_CLIP
    g_ext = rel_bias.astype(jnp.float32)[:, idx].reshape(h, 1, BIAS_W)
    return pl.pallas_call(
        _bias_kernel,
        grid=(h,),
        in_specs=[pl.BlockSpec((None, 1, BIAS_W), lambda i: (i, 0, 0))],
        out_specs=pl.BlockSpec((None, BAND_KW, BAND_G), lambda i: (i, 0, 0)),
        out_shape=jax.ShapeDtypeStruct((h, BAND_KW, BAND_G), jnp.float32),
        compiler_params=_params(("parallel",)),
        name="bias",
    )(g_ext)


def _layer_norm(z, g, b):
    mu = jnp.mean(z, axis=-1, keepdims=True)
    zc = z - mu
    var = jnp.mean(zc * zc, axis=-1, keepdims=True)
    return zc * lax.rsqrt(var + LN_EPS) * g + b


def _gelu_tanh(x):
    return 0.5 * x * (1.0 + jnp.tanh(math.sqrt(2.0 / math.pi) * (x + 0.044715 * (x * x * x))))


def _causal_conv(u, carry, w, bias):
    tm = u.shape[0]
    r8 = lax.broadcasted_iota(jnp.int32, (8, u.shape[1]), 0)
    outs = w[2:3, :] * u + bias
    for shift, j in ((1, 1), (2, 0)):
        ur = pltpu.roll(u, shift, 0)
        cr = pltpu.roll(carry, shift, 0)
        head = jnp.where(r8 < shift, cr, ur[0:8, :])
        us = jnp.concatenate([head, ur[8:tm, :]], axis=0)
        outs = outs + w[j:j + 1, :] * us
    return outs


def _ffn_kernel(ya_ref, yb_ref, x_ref, wout_ref, g1_ref, b1_ref, wup_ref, cw_ref, cb_ref, wdn_ref, g2_ref, b2_ref,
                o_ref, carry_ref, h_ref):
    i = pl.program_id(1)

    @pl.when(i == 0)
    def _():
        carry_ref[...] = jnp.zeros_like(carry_ref)

    y = jnp.concatenate([ya_ref[...], yb_ref[...]], axis=1)
    mix = jnp.dot(y, wout_ref[...], preferred_element_type=jnp.float32)
    x1 = _layer_norm(DEEPNORM_ALPHA * x_ref[...] + mix, g1_ref[...], b1_ref[...])
    xb = x1.astype(jnp.bfloat16)
    tm = x1.shape[0]
    for c in range(D_FF // FFN_CW):
        halves = []
        for off in (c * FFN_CW, D_FF + c * FFN_CW):
            sl = slice(off, off + FFN_CW)
            u = jnp.dot(xb, wup_ref[:, sl], preferred_element_type=jnp.float32)
            halves.append(_causal_conv(u, carry_ref[:, sl], cw_ref[:, sl], cb_ref[:, sl]))
            carry_ref[:, sl] = u[tm - 8:tm, :]
        h_ref[:, c * FFN_CW:(c + 1) * FFN_CW] = (halves[0] * _gelu_tanh(halves[1])).astype(jnp.bfloat16)
    ffn = jnp.dot(h_ref[...], wdn_ref[...], preferred_element_type=jnp.float32)
    o_ref[...] = _layer_norm(DEEPNORM_ALPHA * x1 + ffn, g2_ref[...], b2_ref[...])


def _ffn(ya, yb, x, w_out, g1, b1, w_up, conv_w, conv_b, w_down, g2, b2):
    bsz, s, d = x.shape
    tm = FFN_TM
    nu = w_up.shape[1]
    const = lambda shape: pl.BlockSpec(shape, lambda bi, i: (0, 0), pipeline_mode=pl.Buffered(1))
    rows = lambda width: pl.BlockSpec((None, tm, width), lambda bi, i: (bi, i, 0))
    return pl.pallas_call(
        _ffn_kernel,
        grid=(bsz, s // tm),
        in_specs=[
            rows(ya.shape[2]), rows(yb.shape[2]), rows(d),
            const((d, d)), const((1, d)), const((1, d)),
            const((d, nu)), const((3, nu)), const((1, nu)), const((D_FF, d)), const((1, d)), const((1, d)),
        ],
        out_specs=rows(d),
        out_shape=jax.ShapeDtypeStruct((bsz, s, d), jnp.float32),
        scratch_shapes=[
            pltpu.VMEM((8, nu), jnp.float32),
            pltpu.VMEM((tm, D_FF), jnp.bfloat16),
        ],
        compiler_params=_params(("parallel", "arbitrary")),
        name="ffn",
    )(ya, yb, x, w_out, g1, b1, w_up, conv_w, conv_b, w_down, g2, b2)


def _layer(x, w_in, b_forget, rel_bias, w_out, ln1_g, ln1_b, w_up, conv_w, conv_b, w_down, ln2_g, ln2_b):
    b, s, d = x.shape
    bf = jnp.bfloat16
    fcol = 3 * D_A
    bcol = fcol + N_HEADS_A
    wqk = jnp.concatenate([w_in[:, 0:2 * D_A], w_in[:, bcol:bcol + 2 * D_B]], axis=1).astype(bf)
    wvt = jnp.concatenate([w_in[:, 2 * D_A:fcol], w_in[:, bcol + 2 * D_B:]], axis=1).T.astype(bf)
    wf = jnp.pad(w_in[:, fcol:bcol], ((0, 0), (0, LANES - N_HEADS_A))).astype(bf)
    b_row = jnp.pad(b_forget, (0, LANES - N_HEADS_A)).reshape(1, LANES)

    qk, vt, f = _proj(x, wqk, wvt, wf)
    kaug = _gate(f, b_row)
    ya = _fox(qk, kaug, vt)
    yb = _band(qk, vt, _band_bias(rel_bias))
    return _ffn(ya, yb, x, w_out.astype(bf), ln1_g.reshape(1, d), ln1_b.reshape(1, d),
                w_up.astype(bf), conv_w, conv_b.reshape(1, -1), w_down.astype(bf),
                ln2_g.reshape(1, d), ln2_b.reshape(1, d))


def kernel(x, w_in, b_forget, rel_bias, w_out, ln1_g, ln1_b, w_up, conv_w, conv_b, w_down, ln2_g, ln2_b):
    for l in range(DEPTH):
        x = _layer(x, w_in[l], b_forget[l], rel_bias[l], w_out[l], ln1_g[l], ln1_b[l], w_up[l],
                   conv_w[l], conv_b[l], w_down[l], ln2_g[l], ln2_b[l])
    return x
```

```python
import math

import jax
import jax.numpy as jnp
import numpy as np
from jax import lax
from jax.experimental import pallas as pl
from jax.experimental.pallas import tpu as pltpu

D_MODEL = 1024
HEAD_DIM = 64
N_HEADS_A = 8
N_HEADS_B = 8
D_A = N_HEADS_A * HEAD_DIM
D_B = N_HEADS_B * HEAD_DIM
CHUNK = 64
N_LEFT_CHUNKS = 8
REL_CLIP = 128
D_FF = 2816
LN_EPS = 1e-5
DEPTH = 1
DEEPNORM_ALPHA = (2.0 * DEPTH) ** 0.25

LANES = 128
PAIR = 2 * HEAD_DIM
N_PAIRS_A = D_A // PAIR
N_PAIRS_B = D_B // PAIR
ONES_ROWS = 16
NEG = -1e30
LOG2E = math.log2(math.e)
VMEM_LIMIT = 52 * 1024 * 1024

PROJ_TM = 512
FOX_TQ = 1024
FOX_TK = 512
FOX_STRIP = 256
BAND_HALO = N_LEFT_CHUNKS * CHUNK
BAND_TQ = 1024
BAND_G = 256
BAND_KW = BAND_G + BAND_HALO
BAND_SKEW = 2
FFN_TM = 512
FFN_CW = 256

_NT = (((1,), (1,)), ((), ()))


def _params(sem, flags=None):
    return pltpu.CompilerParams(dimension_semantics=sem, vmem_limit_bytes=VMEM_LIMIT, flags=flags)


def _split3(v):
    h1 = v.astype(jnp.bfloat16).astype(jnp.float32)
    r1 = v - h1
    h2 = r1.astype(jnp.bfloat16).astype(jnp.float32)
    h3 = r1 - h2
    return h1, h2, h3


def _proj_kernel(x_ref, wqk_ref, wvt_ref, wf_ref, b_ref, tri_ref, place_ref, qk_ref, vt_ref, ka_ref, carry_ref):
    i = pl.program_id(1)

    @pl.when(i == 0)
    def _():
        carry_ref[...] = jnp.zeros_like(carry_ref)

    xb = x_ref[...].astype(jnp.bfloat16)
    tm = xb.shape[0]
    z = jnp.dot(xb, wf_ref[...], preferred_element_type=jnp.float32) + b_ref[...]
    logf = -(jnp.maximum(-z, 0.0) + jnp.log1p(jnp.exp(-jnp.abs(z))))
    parts = jnp.concatenate(_split3(logf), axis=1).astype(jnp.bfloat16)

    qk = jnp.dot(xb, wqk_ref[...], preferred_element_type=jnp.float32)
    scale = HEAD_DIM ** -0.5 * LOG2E
    col = lax.broadcasted_iota(jnp.int32, (1, qk.shape[1]), 1)
    is_q = (col < D_A) | ((col >= 2 * D_A) & (col < 2 * D_A + D_B))
    qk_ref[...] = (qk * jnp.where(is_q, scale, 1.0)).astype(jnp.bfloat16)

    sums = jnp.dot(tri_ref[...], parts, preferred_element_type=jnp.float32)
    cum = carry_ref[0:1, :] + (sums[:, 0:LANES] + sums[:, LANES:2 * LANES] + sums[:, 2 * LANES:])
    carry_ref[...] = jnp.broadcast_to(cum[tm - 1:tm, :], carry_ref.shape)
    neg = cum * -LOG2E
    nparts = jnp.concatenate(_split3(neg), axis=1).astype(jnp.bfloat16)

    vt = lax.dot_general(wvt_ref[...], xb, _NT, preferred_element_type=jnp.float32)
    vt_ref[...] = vt.astype(jnp.bfloat16)

    ka = jnp.dot(nparts, place_ref[...], preferred_element_type=jnp.float32).astype(jnp.bfloat16)
    for hp in range(N_PAIRS_A):
        ka_ref[hp] = ka[:, hp * LANES:(hp + 1) * LANES]


def _gate_constants():
    tri = np.tril(np.ones((PROJ_TM, PROJ_TM), np.float32))
    place = np.zeros((3 * LANES, N_PAIRS_A * LANES), np.float32)
    for hp in range(N_PAIRS_A):
        for j in range(2):
            for t in range(3):
                place[t * LANES + 2 * hp + j, hp * LANES + 3 * j + t] = 1.0
    return jnp.asarray(tri, jnp.bfloat16), jnp.asarray(place, jnp.bfloat16)


def _proj(x, wqk, wvt, wf, b_row):
    b, s, d = x.shape
    tm = PROJ_TM
    nqk = wqk.shape[1]
    nv = wvt.shape[0]
    tri, place = _gate_constants()
    const = lambda arr: pl.BlockSpec(arr.shape, lambda bi, i: (0, 0))
    return pl.pallas_call(
        _proj_kernel,
        grid=(b, s // tm),
        in_specs=[
            pl.BlockSpec((None, tm, d), lambda bi, i: (bi, i, 0)),
            const(wqk), const(wvt), const(wf), const(b_row), const(tri), const(place),
        ],
        out_specs=[
            pl.BlockSpec((None, tm, nqk), lambda bi, i: (bi, i, 0)),
            pl.BlockSpec((None, nv, tm), lambda bi, i: (bi, 0, i)),
            pl.BlockSpec((None, N_PAIRS_A, tm, LANES), lambda bi, i: (bi, 0, i, 0)),
        ],
        out_shape=[
            jax.ShapeDtypeStruct((b, s, nqk), jnp.bfloat16),
            jax.ShapeDtypeStruct((b, nv, s), jnp.bfloat16),
            jax.ShapeDtypeStruct((b, N_PAIRS_A, s, LANES), jnp.bfloat16),
        ],
        scratch_shapes=[pltpu.VMEM((8, LANES), jnp.float32)],
        compiler_params=_params(("parallel", "arbitrary")),
        name="proj",
    )(x, wqk, wvt, wf, b_row, tri, place)


def _split_heads(q2):
    lane = lax.broadcasted_iota(jnp.int32, q2.shape, 1)
    qf = q2.astype(jnp.float32)
    even = jnp.where(lane < HEAD_DIM, qf, 0.0).astype(jnp.bfloat16)
    odd = jnp.where(lane < HEAD_DIM, 0.0, qf).astype(jnp.bfloat16)
    return even, odd


def _finalize(acc_e, acc_o):
    oe = acc_e[0:HEAD_DIM, :] / acc_e[HEAD_DIM:HEAD_DIM + 1, :]
    oo = acc_o[0:HEAD_DIM, :] / acc_o[HEAD_DIM:HEAD_DIM + 1, :]
    return jnp.concatenate([oe, oo], axis=0).T


def _vt_with_ones(vt_pair, h):
    ones_rows = jnp.ones((ONES_ROWS, vt_pair.shape[1]), jnp.bfloat16)
    return jnp.concatenate([vt_pair[h * HEAD_DIM:(h + 1) * HEAD_DIM, :], ones_rows], axis=0)


def _fox_kernel(q_ref, k_ref, ka_ref, vt_ref, o_ref, qe_ref, qo_ref, sa_ref, sb_ref, ma_ref, mb_ref,
                acc_e, acc_o, m_e, m_o):
    qi = pl.program_id(2)
    tq, tk = FOX_TQ, FOX_TK
    half = tq // 2
    lane = lax.broadcasted_iota(jnp.int32, (tq, LANES), 1)
    qe_ref[:, 0:LANES], qo_ref[:, 0:LANES] = _split_heads(q_ref[...])
    qe_ref[:, LANES:] = jnp.where(lane < 3, 1.0, 0.0).astype(jnp.bfloat16)
    qo_ref[:, LANES:] = jnp.where((lane >= 3) & (lane < 6), 1.0, 0.0).astype(jnp.bfloat16)
    acc_e[...] = jnp.zeros_like(acc_e)
    acc_o[...] = jnp.zeros_like(acc_o)
    m_e[...] = jnp.full_like(m_e, NEG)
    m_o[...] = jnp.full_like(m_o, NEG)

    q_refs = (qe_ref, qo_ref)
    stats = ((acc_e, m_e), (acc_o, m_o))

    def key_operands(kb):
        ks = pl.multiple_of(kb * tk, tk)
        kcat = jnp.concatenate([k_ref[pl.ds(ks, tk), :], ka_ref[pl.ds(ks, tk), :]], axis=1)
        return kcat, vt_ref[:, pl.ds(ks, tk)]

    def score_strip(kcat, h, c, s_ref, mx_ref):
        cs = slice(c, c + FOX_STRIP)
        st = lax.dot_general(kcat, q_refs[h][cs, :], _NT, preferred_element_type=jnp.float32)
        s_ref[h, :, cs] = st
        mx_ref[h, :, cs] = jnp.max(st, axis=0, keepdims=True)

    tri_r = lax.broadcasted_iota(jnp.int32, (FOX_STRIP, FOX_STRIP), 0)
    tri_c = lax.broadcasted_iota(jnp.int32, (FOX_STRIP, FOX_STRIP), 1)
    tri = jnp.where(tri_r <= tri_c, 0.0, NEG)

    def update_strip(vt_h, h, c, s_ref, mx_ref, mask_c0=None):
        acc, m = stats[h]
        cs = slice(c, c + FOX_STRIP)
        if mask_c0 is None:
            st = s_ref[h, :, cs]
            blk_max = mx_ref[h, :, cs]
        else:
            d = c - mask_c0
            diag = s_ref[h, d:d + FOX_STRIP, cs] + tri
            st = diag if d == 0 else jnp.concatenate([s_ref[h, 0:d, cs], diag], axis=0)
            vt_h = vt_h[:, 0:d + FOX_STRIP]
            blk_max = jnp.max(st, axis=0, keepdims=True)
        m_old = m[:, cs]
        m_new = jnp.maximum(m_old, blk_max)
        p = jnp.exp2(st - m_new).astype(jnp.bfloat16)
        pv = jnp.dot(vt_h, p, preferred_element_type=jnp.float32)
        acc[:, cs] = acc[:, cs] * jnp.exp2(m_old - m_new) + pv
        m[:, cs] = m_new

    def scores(kb, s_ref, mx_ref, c0=0, ncol=tq):
        kcat, _ = key_operands(kb)
        for h in range(2):
            for c in range(c0, c0 + ncol, FOX_STRIP):
                score_strip(kcat, h, c, s_ref, mx_ref)

    def step(kb_next, s_next, mx_next, kb_cur, s_cur, mx_cur):
        kcat, _ = key_operands(kb_next)
        _, vt_pair = key_operands(kb_cur)
        for h in range(2):
            vt_h = _vt_with_ones(vt_pair, h)
            for c in range(0, tq, FOX_STRIP):
                score_strip(kcat, h, c, s_next, mx_next)
                update_strip(vt_h, h, c, s_cur, mx_cur)

    kb0 = 2 * qi
    scores(0, sa_ref, ma_ref)

    def body(j, carry):
        step(2 * j + 1, sb_ref, mb_ref, 2 * j, sa_ref, ma_ref)
        step(2 * j + 2, sa_ref, ma_ref, 2 * j + 1, sb_ref, mb_ref)
        return carry

    lax.fori_loop(0, qi, body, 0)
    kcat1, vt_pair1 = key_operands(kb0 + 1)
    _, vt_pair0 = key_operands(kb0)
    for h in range(2):
        vt_h = _vt_with_ones(vt_pair0, h)
        for c in range(half, tq, FOX_STRIP):
            score_strip(kcat1, h, c, sb_ref, mb_ref)
            update_strip(vt_h, h, c, sa_ref, ma_ref)
        for c in range(0, half, FOX_STRIP):
            update_strip(vt_h, h, c, sa_ref, ma_ref, mask_c0=0)

    for h in range(2):
        vt_h = _vt_with_ones(vt_pair1, h)
        for c in range(half, tq, FOX_STRIP):
            update_strip(vt_h, h, c, sb_ref, mb_ref, mask_c0=half)
    o_ref[...] = _finalize(acc_e[...], acc_o[...]).astype(o_ref.dtype)


def _fox(qk, kaug, vt):
    b, s, _ = qk.shape
    tq, tk = FOX_TQ, FOX_TK
    kcol0 = D_A // LANES
    return pl.pallas_call(
        _fox_kernel,
        grid=(b, N_PAIRS_A, s // tq),
        in_specs=[
            pl.BlockSpec((None, tq, LANES), lambda bi, hp, qi: (bi, qi, hp)),
            pl.BlockSpec((None, s, LANES), lambda bi, hp, qi: (bi, 0, kcol0 + hp)),
            pl.BlockSpec((None, None, s, LANES), lambda bi, hp, qi: (bi, hp, 0, 0)),
            pl.BlockSpec((None, LANES, s), lambda bi, hp, qi: (bi, hp, 0)),
        ],
        out_specs=pl.BlockSpec((None, tq, LANES), lambda bi, hp, qi: (bi, qi, hp)),
        out_shape=jax.ShapeDtypeStruct((b, s, D_A), jnp.bfloat16),
        scratch_shapes=[
            pltpu.VMEM((tq, 2 * LANES), jnp.bfloat16),
            pltpu.VMEM((tq, 2 * LANES), jnp.bfloat16),
            pltpu.VMEM((2, tk, tq), jnp.float32),
            pltpu.VMEM((2, tk, tq), jnp.float32),
            pltpu.VMEM((2, 1, tq), jnp.float32),
            pltpu.VMEM((2, 1, tq), jnp.float32),
            pltpu.VMEM((HEAD_DIM + ONES_ROWS, tq), jnp.float32),
            pltpu.VMEM((HEAD_DIM + ONES_ROWS, tq), jnp.float32),
            pltpu.VMEM((1, tq), jnp.float32),
            pltpu.VMEM((1, tq), jnp.float32),
        ],
        compiler_params=_params(("parallel", "parallel", "arbitrary")),
        name="fox",
    )(qk, qk, kaug, vt)


def _band_kernel(q_ref, kp_ref, kc_ref, vtp_ref, vtc_ref, bias_ref, o_ref, kwin, vtwin, s_scr, p_scr):
    i = pl.program_id(2)
    tq, halo = BAND_TQ, BAND_HALO
    n_units = 2 * (tq // BAND_G)
    lane = lax.broadcasted_iota(jnp.int32, (halo, LANES), 1)
    kwin[0:halo, 0:LANES] = kp_ref[...]
    kwin[halo:, 0:LANES] = kc_ref[...]
    pen = jnp.where(i == 0, NEG, 0.0)
    kwin[0:halo, LANES:] = jnp.where(lane == 0, pen, 0.0).astype(jnp.bfloat16)
    kwin[halo:, LANES:] = jnp.zeros((tq, LANES), jnp.bfloat16)
    vtwin[:, 0:halo] = vtp_ref[...]
    vtwin[:, halo:] = vtc_ref[...]
    lane_g = lax.broadcasted_iota(jnp.int32, (BAND_G, LANES), 1)
    ones_col = jnp.where(lane_g == 0, 1.0, 0.0).astype(jnp.bfloat16)

    def scores(u):
        g, h = divmod(u, 2)
        qh = _split_heads(q_ref[g * BAND_G:(g + 1) * BAND_G, :])[h]
        qcat = jnp.concatenate([qh, ones_col], axis=1)
        kslab = kwin[g * BAND_G:g * BAND_G + BAND_KW, :]
        s_scr[u] = lax.dot_general(kslab, qcat, _NT, preferred_element_type=jnp.float32)

    def softmax(u):
        st = s_scr[u] + bias_ref[u % 2]
        m = jnp.max(st, axis=0, keepdims=True)
        p_scr[u] = jnp.exp2(st - m).astype(jnp.bfloat16)

    accs = {}

    def pv(u):
        g, h = divmod(u, 2)
        vslab = vtwin[:, g * BAND_G:g * BAND_G + BAND_KW]
        accs[h] = jnp.dot(_vt_with_ones(vslab, h), p_scr[u], preferred_element_type=jnp.float32)
        if h == 1:
            o_ref[g * BAND_G:(g + 1) * BAND_G, :] = _finalize(accs[0], accs[1]).astype(o_ref.dtype)

    for t in range(n_units + 2 * BAND_SKEW):
        if t < n_units:
            scores(t)
        if BAND_SKEW <= t < n_units + BAND_SKEW:
            softmax(t - BAND_SKEW)
        if t >= 2 * BAND_SKEW:
            pv(t - 2 * BAND_SKEW)


def _band(qk, vt, bias_t):
    b, s, _ = qk.shape
    tq, halo = BAND_TQ, BAND_HALO
    r = tq // halo
    qcol0 = 2 * D_A // LANES
    kcol0 = (2 * D_A + D_B) // LANES
    vrow0 = D_A // PAIR
    prev = lambda i: jnp.maximum(i * r - 1, 0)
    n_units = 2 * (tq // BAND_G)
    return pl.pallas_call(
        _band_kernel,
        grid=(b, N_PAIRS_B, s // tq),
        in_specs=[
            pl.BlockSpec((None, tq, LANES), lambda bi, hp, i: (bi, i, qcol0 + hp)),
            pl.BlockSpec((None, halo, LANES), lambda bi, hp, i: (bi, prev(i), kcol0 + hp)),
            pl.BlockSpec((None, tq, LANES), lambda bi, hp, i: (bi, i, kcol0 + hp)),
            pl.BlockSpec((None, PAIR, halo), lambda bi, hp, i: (bi, vrow0 + hp, prev(i))),
            pl.BlockSpec((None, PAIR, tq), lambda bi, hp, i: (bi, vrow0 + hp, i)),
            pl.BlockSpec((2, BAND_KW, BAND_G), lambda bi, hp, i: (hp, 0, 0)),
        ],
        out_specs=pl.BlockSpec((None, tq, LANES), lambda bi, hp, i: (bi, i, hp)),
        out_shape=jax.ShapeDtypeStruct((b, s, D_B), jnp.bfloat16),
        scratch_shapes=[
            pltpu.VMEM((halo + tq, 2 * LANES), jnp.bfloat16),
            pltpu.VMEM((PAIR, halo + tq), jnp.bfloat16),
            pltpu.VMEM((n_units, BAND_KW, BAND_G), jnp.float32),
            pltpu.VMEM((n_units, BAND_KW, BAND_G), jnp.bfloat16),
        ],
        compiler_params=_params(("parallel", "parallel", "arbitrary")),
        name="band",
    )(qk, qk, qk, vt, vt, bias_t)


BIAS_W = 1024


def _bias_kernel(g_ref, o_ref):
    row = jnp.broadcast_to(g_ref[...], (BAND_KW, BIAS_W))
    toe = pltpu.roll(row, 0, 1, stride=1, stride_axis=0)[:, 0:BAND_G]
    kc = lax.broadcasted_iota(jnp.int32, (BAND_KW, BAND_G), 0) // CHUNK
    qc = lax.broadcasted_iota(jnp.int32, (BAND_KW, BAND_G), 1) // CHUNK
    inband = (kc >= qc) & (kc <= qc + N_LEFT_CHUNKS)
    o_ref[...] = jnp.where(inband, toe * LOG2E, NEG)


def _band_bias(rel_bias):
    h = rel_bias.shape[0]
    j = np.arange(BIAS_W)
    off = np.where(j < BAND_G, j, j - BIAS_W)
    idx = np.clip(N_LEFT_CHUNKS * CHUNK + off, -REL_CLIP, REL_CLIP) + REL_CLIP
    g_ext = rel_bias.astype(jnp.float32)[:, idx].reshape(h, 1, BIAS_W)
    return pl.pallas_call(
        _bias_kernel,
        grid=(h,),
        in_specs=[pl.BlockSpec((None, 1, BIAS_W), lambda i: (i, 0, 0))],
        out_specs=pl.BlockSpec((None, BAND_KW, BAND_G), lambda i: (i, 0, 0)),
        out_shape=jax.ShapeDtypeStruct((h, BAND_KW, BAND_G), jnp.float32),
        compiler_params=_params(("parallel",)),
        name="bias",
    )(g_ext)


def _layer_norm(z, g, b):
    mu = jnp.mean(z, axis=-1, keepdims=True)
    zc = z - mu
    var = jnp.mean(zc * zc, axis=-1, keepdims=True)
    return zc * lax.rsqrt(var + LN_EPS) * g + b


def _gelu_tanh(x):
    return 0.5 * x * (1.0 + jnp.tanh(math.sqrt(2.0 / math.pi) * (x + 0.044715 * (x * x * x))))


def _causal_conv(u, carry, w, bias):
    tm = u.shape[0]
    r8 = lax.broadcasted_iota(jnp.int32, (8, u.shape[1]), 0)
    outs = w[2:3, :] * u + bias
    for shift, j in ((1, 1), (2, 0)):
        ur = pltpu.roll(u, shift, 0)
        cr = pltpu.roll(carry, shift, 0)
        head = jnp.where(r8 < shift, cr, ur[0:8, :])
        us = jnp.concatenate([head, ur[8:tm, :]], axis=0)
        outs = outs + w[j:j + 1, :] * us
    return outs


def _ffn_kernel(ya_ref, yb_ref, x_ref, wout_ref, g1_ref, b1_ref, wup_ref, cw_ref, cb_ref, wdn_ref, g2_ref, b2_ref,
                o_ref, carry_ref, h_ref):
    i = pl.program_id(1)

    @pl.when(i == 0)
    def _():
        carry_ref[...] = jnp.zeros_like(carry_ref)

    y = jnp.concatenate([ya_ref[...], yb_ref[...]], axis=1)
    mix = jnp.dot(y, wout_ref[...], preferred_element_type=jnp.float32)
    x1 = _layer_norm(DEEPNORM_ALPHA * x_ref[...] + mix, g1_ref[...], b1_ref[...])
    xb = x1.astype(jnp.bfloat16)
    tm = x1.shape[0]
    for c in range(D_FF // FFN_CW):
        halves = []
        for off in (c * FFN_CW, D_FF + c * FFN_CW):
            sl = slice(off, off + FFN_CW)
            u = jnp.dot(xb, wup_ref[:, sl], preferred_element_type=jnp.float32)
            halves.append(_causal_conv(u, carry_ref[:, sl], cw_ref[:, sl], cb_ref[:, sl]))
            carry_ref[:, sl] = u[tm - 8:tm, :]
        h_ref[:, c * FFN_CW:(c + 1) * FFN_CW] = (halves[0] * _gelu_tanh(halves[1])).astype(jnp.bfloat16)
    ffn = jnp.dot(h_ref[...], wdn_ref[...], preferred_element_type=jnp.float32)
    o_ref[...] = _layer_norm(DEEPNORM_ALPHA * x1 + ffn, g2_ref[...], b2_ref[...])


def _ffn(ya, yb, x, w_out, g1, b1, w_up, conv_w, conv_b, w_down, g2, b2):
    bsz, s, d = x.shape
    tm = FFN_TM
    nu = w_up.shape[1]
    const = lambda shape: pl.BlockSpec(shape, lambda bi, i: (0, 0), pipeline_mode=pl.Buffered(1))
    rows = lambda width: pl.BlockSpec((None, tm, width), lambda bi, i: (bi, i, 0))
    return pl.pallas_call(
        _ffn_kernel,
        grid=(bsz, s // tm),
        in_specs=[
            rows(ya.shape[2]), rows(yb.shape[2]), rows(d),
            const((d, d)), const((1, d)), const((1, d)),
            const((d, nu)), const((3, nu)), const((1, nu)), const((D_FF, d)), const((1, d)), const((1, d)),
        ],
        out_specs=rows(d),
        out_shape=jax.ShapeDtypeStruct((bsz, s, d), jnp.float32),
        scratch_shapes=[
            pltpu.VMEM((8, nu), jnp.float32),
            pltpu.VMEM((tm, D_FF), jnp.bfloat16),
        ],
        compiler_params=_params(("parallel", "arbitrary")),
        name="ffn",
    )(ya, yb, x, w_out, g1, b1, w_up, conv_w, conv_b, w_down, g2, b2)


def _layer(x, w_in, b_forget, rel_bias, w_out, ln1_g, ln1_b, w_up, conv_w, conv_b, w_down, ln2_g, ln2_b):
    b, s, d = x.shape
    bf = jnp.bfloat16
    fcol = 3 * D_A
    bcol = fcol + N_HEADS_A
    wqk = jnp.concatenate([w_in[:, 0:2 * D_A], w_in[:, bcol:bcol + 2 * D_B]], axis=1).astype(bf)
    wvt = jnp.concatenate([w_in[:, 2 * D_A:fcol], w_in[:, bcol + 2 * D_B:]], axis=1).T.astype(bf)
    wf = jnp.pad(w_in[:, fcol:bcol], ((0, 0), (0, LANES - N_HEADS_A))).astype(bf)
    b_row = jnp.pad(b_forget, (0, LANES - N_HEADS_A)).reshape(1, LANES)

    qk, vt, kaug = _proj(x, wqk, wvt, wf, b_row)
    ya = _fox(qk, kaug, vt)
    yb = _band(qk, vt, _band_bias(rel_bias))
    return _ffn(ya, yb, x, w_out.astype(bf), ln1_g.reshape(1, d), ln1_b.reshape(1, d),
                w_up.astype(bf), conv_w, conv_b.reshape(1, -1), w_down.astype(bf),
                ln2_g.reshape(1, d), ln2_b.reshape(1, d))


def kernel(x, w_in, b_forget, rel_bias, w_out, ln1_g, ln1_b, w_up, conv_w, conv_b, w_down, ln2_g, ln2_b):
    for l in range(DEPTH):
        x = _layer(x, w_in[l], b_forget[l], rel_bias[l], w_out[l], ln1_g[l], ln1_b[l], w_up[l],
                   conv_w[l], conv_b[l], w_down[l], ln2_g[l], ln2_b[l])
    return x
```

```python
import math

import jax
import jax.numpy as jnp
import numpy as np
from jax import lax
from jax.experimental import pallas as pl
from jax.experimental.pallas import tpu as pltpu

D_MODEL = 1024
HEAD_DIM = 64
N_HEADS_A = 8
N_HEADS_B = 8
D_A = N_HEADS_A * HEAD_DIM
D_B = N_HEADS_B * HEAD_DIM
CHUNK = 64
N_LEFT_CHUNKS = 8
REL_CLIP = 128
D_FF = 2816
LN_EPS = 1e-5
DEPTH = 1
DEEPNORM_ALPHA = (2.0 * DEPTH) ** 0.25

LANES = 128
PAIR = 2 * HEAD_DIM
N_PAIRS_A = D_A // PAIR
N_PAIRS_B = D_B // PAIR
ONES_ROWS = 16
NEG = -1e30
LOG2E = math.log2(math.e)
VMEM_LIMIT = 52 * 1024 * 1024

PROJ_TM = 512
FOX_TQ = 1024
FOX_TK = 512
FOX_STRIP = 256
BAND_HALO = N_LEFT_CHUNKS * CHUNK
BAND_TQ = 1024
BAND_G = 256
BAND_KW = BAND_G + BAND_HALO
BAND_SKEW = 2
FFN_TM = 512
FFN_CW = 256


def _params(sem, flags=None):
    return pltpu.CompilerParams(dimension_semantics=sem, vmem_limit_bytes=VMEM_LIMIT, flags=flags)


def _split3(v):
    h1 = v.astype(jnp.bfloat16).astype(jnp.float32)
    r1 = v - h1
    h2 = r1.astype(jnp.bfloat16).astype(jnp.float32)
    h3 = r1 - h2
    return h1, h2, h3


def _proj_kernel(x_ref, wqk_ref, wv_ref, wf_ref, b_ref, tri_ref, place_ref, qk_ref, vt_ref, ka_ref, carry_ref):
    i = pl.program_id(1)

    @pl.when(i == 0)
    def _():
        carry_ref[...] = jnp.zeros_like(carry_ref)

    xb = x_ref[...].astype(jnp.bfloat16)
    tm = xb.shape[0]
    z = jnp.dot(xb, wf_ref[...], preferred_element_type=jnp.float32) + b_ref[...]
    logf = -(jnp.maximum(-z, 0.0) + jnp.log1p(jnp.exp(-jnp.abs(z))))
    parts = jnp.concatenate(_split3(logf), axis=1).astype(jnp.bfloat16)

    qk = jnp.dot(xb, wqk_ref[...], preferred_element_type=jnp.float32)
    scale = HEAD_DIM ** -0.5 * LOG2E
    col = lax.broadcasted_iota(jnp.int32, (1, qk.shape[1]), 1)
    is_q = (col < D_A) | ((col >= 2 * D_A) & (col < 2 * D_A + D_B))
    qk_ref[...] = (qk * jnp.where(is_q, scale, 1.0)).astype(jnp.bfloat16)

    sums = jnp.dot(tri_ref[...], parts, preferred_element_type=jnp.float32)
    cum = carry_ref[0:1, :] + (sums[:, 0:LANES] + sums[:, LANES:2 * LANES] + sums[:, 2 * LANES:])
    carry_ref[...] = jnp.broadcast_to(cum[tm - 1:tm, :], carry_ref.shape)
    neg = cum * -LOG2E
    nparts = jnp.concatenate(_split3(neg), axis=1).astype(jnp.bfloat16)

    v = jnp.dot(xb, wv_ref[...], preferred_element_type=jnp.float32)
    vt_ref[...] = v.T.astype(jnp.bfloat16)

    ka = jnp.dot(nparts, place_ref[...], preferred_element_type=jnp.float32).astype(jnp.bfloat16)
    for hp in range(N_PAIRS_A):
        ka_ref[hp] = ka[:, hp * LANES:(hp + 1) * LANES]


def _gate_constants():
    tri = np.tril(np.ones((PROJ_TM, PROJ_TM), np.float32))
    place = np.zeros((3 * LANES, N_PAIRS_A * LANES), np.float32)
    for hp in range(N_PAIRS_A):
        for j in range(2):
            for t in range(3):
                place[t * LANES + 2 * hp + j, hp * LANES + 3 * j + t] = 1.0
    return jnp.asarray(tri, jnp.bfloat16), jnp.asarray(place, jnp.bfloat16)


def _proj(x, wqk, wv, wf, b_row):
    b, s, d = x.shape
    tm = PROJ_TM
    nqk = wqk.shape[1]
    nv = wv.shape[1]
    tri, place = _gate_constants()
    const = lambda arr: pl.BlockSpec(arr.shape, lambda bi, i: (0, 0))
    return pl.pallas_call(
        _proj_kernel,
        grid=(b, s // tm),
        in_specs=[
            pl.BlockSpec((None, tm, d), lambda bi, i: (bi, i, 0)),
            const(wqk), const(wv), const(wf), const(b_row), const(tri), const(place),
        ],
        out_specs=[
            pl.BlockSpec((None, tm, nqk), lambda bi, i: (bi, i, 0)),
            pl.BlockSpec((None, nv, tm), lambda bi, i: (bi, 0, i)),
            pl.BlockSpec((None, N_PAIRS_A, tm, LANES), lambda bi, i: (bi, 0, i, 0)),
        ],
        out_shape=[
            jax.ShapeDtypeStruct((b, s, nqk), jnp.bfloat16),
            jax.ShapeDtypeStruct((b, nv, s), jnp.bfloat16),
            jax.ShapeDtypeStruct((b, N_PAIRS_A, s, LANES), jnp.bfloat16),
        ],
        scratch_shapes=[pltpu.VMEM((8, LANES), jnp.float32)],
        compiler_params=_params(("parallel", "arbitrary")),
        name="proj",
    )(x, wqk, wv, wf, b_row, tri, place)


def _finalize(acc_e, acc_o):
    oe = acc_e[0:HEAD_DIM, :] / acc_e[HEAD_DIM:HEAD_DIM + 1, :]
    oo = acc_o[0:HEAD_DIM, :] / acc_o[HEAD_DIM:HEAD_DIM + 1, :]
    return jnp.concatenate([oe, oo], axis=0).T


def _vt_with_ones(vt_pair, h):
    ones_rows = jnp.ones((ONES_ROWS, vt_pair.shape[1]), jnp.bfloat16)
    return jnp.concatenate([vt_pair[h * HEAD_DIM:(h + 1) * HEAD_DIM, :], ones_rows], axis=0)


def _fox_kernel(q_ref, k_ref, ka_ref, vt_ref, o_ref, qe_ref, qo_ref, sa_ref, sb_ref, ma_ref, mb_ref,
                acc_e, acc_o, m_e, m_o):
    qi = pl.program_id(2)
    tq, tk = FOX_TQ, FOX_TK
    half = tq // 2
    row = lax.broadcasted_iota(jnp.int32, (LANES, tq), 0)
    qt = q_ref[...].astype(jnp.float32).T
    qe_ref[0:LANES, :] = jnp.where(row < HEAD_DIM, qt, 0.0).astype(jnp.bfloat16)
    qo_ref[0:LANES, :] = jnp.where(row < HEAD_DIM, 0.0, qt).astype(jnp.bfloat16)
    qe_ref[LANES:, :] = jnp.where(row < 3, 1.0, 0.0).astype(jnp.bfloat16)
    qo_ref[LANES:, :] = jnp.where((row >= 3) & (row < 6), 1.0, 0.0).astype(jnp.bfloat16)
    acc_e[...] = jnp.zeros_like(acc_e)
    acc_o[...] = jnp.zeros_like(acc_o)
    m_e[...] = jnp.full_like(m_e, NEG)
    m_o[...] = jnp.full_like(m_o, NEG)

    q_refs = (qe_ref, qo_ref)
    stats = ((acc_e, m_e), (acc_o, m_o))

    def key_operands(kb):
        ks = pl.multiple_of(kb * tk, tk)
        kcat = jnp.concatenate([k_ref[pl.ds(ks, tk), :], ka_ref[pl.ds(ks, tk), :]], axis=1)
        return kcat, vt_ref[:, pl.ds(ks, tk)]

    def score_strip(kcat, h, c, s_ref, mx_ref):
        cs = slice(c, c + FOX_STRIP)
        st = jnp.dot(kcat, q_refs[h][:, cs], preferred_element_type=jnp.float32)
        s_ref[h, :, cs] = st
        mx_ref[h, :, cs] = jnp.max(st, axis=0, keepdims=True)

    tri_r = lax.broadcasted_iota(jnp.int32, (FOX_STRIP, FOX_STRIP), 0)
    tri_c = lax.broadcasted_iota(jnp.int32, (FOX_STRIP, FOX_STRIP), 1)
    tri = jnp.where(tri_r <= tri_c, 0.0, NEG)

    def update_strip(vt_h, h, c, s_ref, mx_ref, mask_c0=None):
        acc, m = stats[h]
        cs = slice(c, c + FOX_STRIP)
        if mask_c0 is None:
            st = s_ref[h, :, cs]
            blk_max = mx_ref[h, :, cs]
        else:
            d = c - mask_c0
            diag = s_ref[h, d:d + FOX_STRIP, cs] + tri
            st = diag if d == 0 else jnp.concatenate([s_ref[h, 0:d, cs], diag], axis=0)
            vt_h = vt_h[:, 0:d + FOX_STRIP]
            blk_max = jnp.max(st, axis=0, keepdims=True)
        m_old = m[:, cs]
        m_new = jnp.maximum(m_old, blk_max)
        p = jnp.exp2(st - m_new).astype(jnp.bfloat16)
        pv = jnp.dot(vt_h, p, preferred_element_type=jnp.float32)
        acc[:, cs] = acc[:, cs] * jnp.exp2(m_old - m_new) + pv
        m[:, cs] = m_new

    def scores(kb, s_ref, mx_ref, c0=0, ncol=tq):
        kcat, _ = key_operands(kb)
        for h in range(2):
            for c in range(c0, c0 + ncol, FOX_STRIP):
                score_strip(kcat, h, c, s_ref, mx_ref)

    def step(kb_next, s_next, mx_next, kb_cur, s_cur, mx_cur):
        kcat, _ = key_operands(kb_next)
        _, vt_pair = key_operands(kb_cur)
        for h in range(2):
            vt_h = _vt_with_ones(vt_pair, h)
            for c in range(0, tq, FOX_STRIP):
                score_strip(kcat, h, c, s_next, mx_next)
                update_strip(vt_h, h, c, s_cur, mx_cur)

    kb0 = 2 * qi
    scores(0, sa_ref, ma_ref)

    def body(j, carry):
        step(2 * j + 1, sb_ref, mb_ref, 2 * j, sa_ref, ma_ref)
        step(2 * j + 2, sa_ref, ma_ref, 2 * j + 1, sb_ref, mb_ref)
        return carry

    lax.fori_loop(0, qi, body, 0)
    kcat1, vt_pair1 = key_operands(kb0 + 1)
    _, vt_pair0 = key_operands(kb0)
    for h in range(2):
        vt_h = _vt_with_ones(vt_pair0, h)
        for c in range(half, tq, FOX_STRIP):
            score_strip(kcat1, h, c, sb_ref, mb_ref)
            update_strip(vt_h, h, c, sa_ref, ma_ref)
        for c in range(0, half, FOX_STRIP):
            update_strip(vt_h, h, c, sa_ref, ma_ref, mask_c0=0)

    for h in range(2):
        vt_h = _vt_with_ones(vt_pair1, h)
        for c in range(half, tq, FOX_STRIP):
            update_strip(vt_h, h, c, sb_ref, mb_ref, mask_c0=half)
    o_ref[...] = _finalize(acc_e[...], acc_o[...]).astype(o_ref.dtype)


def _fox(qk, kaug, vt):
    b, s, _ = qk.shape
    tq, tk = FOX_TQ, FOX_TK
    kcol0 = D_A // LANES
    return pl.pallas_call(
        _fox_kernel,
        grid=(b, N_PAIRS_A, s // tq),
        in_specs=[
            pl.BlockSpec((None, tq, LANES), lambda bi, hp, qi: (bi, qi, hp)),
            pl.BlockSpec((None, s, LANES), lambda bi, hp, qi: (bi, 0, kcol0 + hp)),
            pl.BlockSpec((None, None, s, LANES), lambda bi, hp, qi: (bi, hp, 0, 0)),
            pl.BlockSpec((None, LANES, s), lambda bi, hp, qi: (bi, hp, 0)),
        ],
        out_specs=pl.BlockSpec((None, tq, LANES), lambda bi, hp, qi: (bi, qi, hp)),
        out_shape=jax.ShapeDtypeStruct((b, s, D_A), jnp.bfloat16),
        scratch_shapes=[
            pltpu.VMEM((2 * LANES, tq), jnp.bfloat16),
            pltpu.VMEM((2 * LANES, tq), jnp.bfloat16),
            pltpu.VMEM((2, tk, tq), jnp.float32),
            pltpu.VMEM((2, tk, tq), jnp.float32),
            pltpu.VMEM((2, 1, tq), jnp.float32),
            pltpu.VMEM((2, 1, tq), jnp.float32),
            pltpu.VMEM((HEAD_DIM + ONES_ROWS, tq), jnp.float32),
            pltpu.VMEM((HEAD_DIM + ONES_ROWS, tq), jnp.float32),
            pltpu.VMEM((1, tq), jnp.float32),
            pltpu.VMEM((1, tq), jnp.float32),
        ],
        compiler_params=_params(("parallel", "parallel", "arbitrary")),
        name="fox",
    )(qk, qk, kaug, vt)


def _band_kernel(q_ref, kp_ref, kc_ref, vtp_ref, vtc_ref, bias_ref, o_ref, kwin, vtwin, s_scr, p_scr):
    i = pl.program_id(2)
    tq, halo = BAND_TQ, BAND_HALO
    n_units = 2 * (tq // BAND_G)
    lane = lax.broadcasted_iota(jnp.int32, (halo, LANES), 1)
    kwin[0:halo, 0:LANES] = kp_ref[...]
    kwin[halo:, 0:LANES] = kc_ref[...]
    pen = jnp.where(i == 0, NEG, 0.0)
    kwin[0:halo, LANES:] = jnp.where(lane == 0, pen, 0.0).astype(jnp.bfloat16)
    kwin[halo:, LANES:] = jnp.zeros((tq, LANES), jnp.bfloat16)
    vtwin[:, 0:halo] = vtp_ref[...]
    vtwin[:, halo:] = vtc_ref[...]
    row_g = lax.broadcasted_iota(jnp.int32, (LANES, BAND_G), 0)
    ones_row = jnp.where(row_g == 0, 1.0, 0.0).astype(jnp.bfloat16)

    def scores(u):
        g, h = divmod(u, 2)
        qt = q_ref[g * BAND_G:(g + 1) * BAND_G, :].astype(jnp.float32).T
        own = (row_g < HEAD_DIM) if h == 0 else (row_g >= HEAD_DIM)
        qcat = jnp.concatenate([jnp.where(own, qt, 0.0).astype(jnp.bfloat16), ones_row], axis=0)
        kslab = kwin[g * BAND_G:g * BAND_G + BAND_KW, :]
        s_scr[u] = jnp.dot(kslab, qcat, preferred_element_type=jnp.float32)

    def softmax(u):
        st = s_scr[u] + bias_ref[u % 2]
        m = jnp.max(st, axis=0, keepdims=True)
        p_scr[u] = jnp.exp2(st - m).astype(jnp.bfloat16)

    accs = {}

    def pv(u):
        g, h = divmod(u, 2)
        vslab = vtwin[:, g * BAND_G:g * BAND_G + BAND_KW]
        accs[h] = jnp.dot(_vt_with_ones(vslab, h), p_scr[u], preferred_element_type=jnp.float32)
        if h == 1:
            o_ref[g * BAND_G:(g + 1) * BAND_G, :] = _finalize(accs[0], accs[1]).astype(o_ref.dtype)

    for t in range(n_units + 2 * BAND_SKEW):
        if t < n_units:
            scores(t)
        if BAND_SKEW <= t < n_units + BAND_SKEW:
            softmax(t - BAND_SKEW)
        if t >= 2 * BAND_SKEW:
            pv(t - 2 * BAND_SKEW)


def _band(qk, vt, bias_t):
    b, s, _ = qk.shape
    tq, halo = BAND_TQ, BAND_HALO
    r = tq // halo
    qcol0 = 2 * D_A // LANES
    kcol0 = (2 * D_A + D_B) // LANES
    vrow0 = D_A // PAIR
    prev = lambda i: jnp.maximum(i * r - 1, 0)
    n_units = 2 * (tq // BAND_G)
    return pl.pallas_call(
        _band_kernel,
        grid=(b, N_PAIRS_B, s // tq),
        in_specs=[
            pl.BlockSpec((None, tq, LANES), lambda bi, hp, i: (bi, i, qcol0 + hp)),
            pl.BlockSpec((None, halo, LANES), lambda bi, hp, i: (bi, prev(i), kcol0 + hp)),
            pl.BlockSpec((None, tq, LANES), lambda bi, hp, i: (bi, i, kcol0 + hp)),
            pl.BlockSpec((None, PAIR, halo), lambda bi, hp, i: (bi, vrow0 + hp, prev(i))),
            pl.BlockSpec((None, PAIR, tq), lambda bi, hp, i: (bi, vrow0 + hp, i)),
            pl.BlockSpec((2, BAND_KW, BAND_G), lambda bi, hp, i: (hp, 0, 0)),
        ],
        out_specs=pl.BlockSpec((None, tq, LANES), lambda bi, hp, i: (bi, i, hp)),
        out_shape=jax.ShapeDtypeStruct((b, s, D_B), jnp.bfloat16),
        scratch_shapes=[
            pltpu.VMEM((halo + tq, 2 * LANES), jnp.bfloat16),
            pltpu.VMEM((PAIR, halo + tq), jnp.bfloat16),
            pltpu.VMEM((n_units, BAND_KW, BAND_G), jnp.float32),
            pltpu.VMEM((n_units, BAND_KW, BAND_G), jnp.bfloat16),
        ],
        compiler_params=_params(("parallel", "parallel", "arbitrary")),
        name="band",
    )(qk, qk, qk, vt, vt, bias_t)


BIAS_W = 1024


def _bias_kernel(g_ref, o_ref):
    row = jnp.broadcast_to(g_ref[...], (BAND_KW, BIAS_W))
    toe = pltpu.roll(row, 0, 1, stride=1, stride_axis=0)[:, 0:BAND_G]
    kc = lax.broadcasted_iota(jnp.int32, (BAND_KW, BAND_G), 0) // CHUNK
    qc = lax.broadcasted_iota(jnp.int32, (BAND_KW, BAND_G), 1) // CHUNK
    inband = (kc >= qc) & (kc <= qc + N_LEFT_CHUNKS)
    o_ref[...] = jnp.where(inband, toe * LOG2E, NEG)


def _band_bias(rel_bias):
    h = rel_bias.shape[0]
    j = np.arange(BIAS_W)
    off = np.where(j < BAND_G, j, j - BIAS_W)
    idx = np.clip(N_LEFT_CHUNKS * CHUNK + off, -REL_CLIP, REL_CLIP) + REL_CLIP
    g_ext = rel_bias.astype(jnp.float32)[:, idx].reshape(h, 1, BIAS_W)
    return pl.pallas_call(
        _bias_kernel,
        grid=(h,),
        in_specs=[pl.BlockSpec((None, 1, BIAS_W), lambda i: (i, 0, 0))],
        out_specs=pl.BlockSpec((None, BAND_KW, BAND_G), lambda i: (i, 0, 0)),
        out_shape=jax.ShapeDtypeStruct((h, BAND_KW, BAND_G), jnp.float32),
        compiler_params=_params(("parallel",)),
        name="bias",
    )(g_ext)


def _layer_norm(z, g, b):
    mu = jnp.mean(z, axis=-1, keepdims=True)
    zc = z - mu
    var = jnp.mean(zc * zc, axis=-1, keepdims=True)
    return zc * lax.rsqrt(var + LN_EPS) * g + b


def _gelu_tanh(x):
    return 0.5 * x * (1.0 + jnp.tanh(math.sqrt(2.0 / math.pi) * (x + 0.044715 * (x * x * x))))


def _causal_conv(u, carry, w, bias):
    tm = u.shape[0]
    r8 = lax.broadcasted_iota(jnp.int32, (8, u.shape[1]), 0)
    outs = w[2:3, :] * u + bias
    for shift, j in ((1, 1), (2, 0)):
        ur = pltpu.roll(u, shift, 0)
        cr = pltpu.roll(carry, shift, 0)
        head = jnp.where(r8 < shift, cr, ur[0:8, :])
        us = jnp.concatenate([head, ur[8:tm, :]], axis=0)
        outs = outs + w[j:j + 1, :] * us
    return outs


def _ffn_kernel(ya_ref, yb_ref, x_ref, wout_ref, g1_ref, b1_ref, wup_ref, cw_ref, cb_ref, wdn_ref, g2_ref, b2_ref,
                o_ref, carry_ref, h_ref):
    i = pl.program_id(1)

    @pl.when(i == 0)
    def _():
        carry_ref[...] = jnp.zeros_like(carry_ref)

    y = jnp.concatenate([ya_ref[...], yb_ref[...]], axis=1)
    mix = jnp.dot(y, wout_ref[...], preferred_element_type=jnp.float32)
    x1 = _layer_norm(DEEPNORM_ALPHA * x_ref[...] + mix, g1_ref[...], b1_ref[...])
    xb = x1.astype(jnp.bfloat16)
    tm = x1.shape[0]
    for c in range(D_FF // FFN_CW):
        halves = []
        for off in (c * FFN_CW, D_FF + c * FFN_CW):
            sl = slice(off, off + FFN_CW)
            u = jnp.dot(xb, wup_ref[:, sl], preferred_element_type=jnp.float32)
            halves.append(_causal_conv(u, carry_ref[:, sl], cw_ref[:, sl], cb_ref[:, sl]))
            carry_ref[:, sl] = u[tm - 8:tm, :]
        h_ref[:, c * FFN_CW:(c + 1) * FFN_CW] = (halves[0] * _gelu_tanh(halves[1])).astype(jnp.bfloat16)
    ffn = jnp.dot(h_ref[...], wdn_ref[...], preferred_element_type=jnp.float32)
    o_ref[...] = _layer_norm(DEEPNORM_ALPHA * x1 + ffn, g2_ref[...], b2_ref[...])


def _ffn(ya, yb, x, w_out, g1, b1, w_up, conv_w, conv_b, w_down, g2, b2):
    bsz, s, d = x.shape
    tm = FFN_TM
    nu = w_up.shape[1]
    const = lambda shape: pl.BlockSpec(shape, lambda bi, i: (0, 0), pipeline_mode=pl.Buffered(1))
    rows = lambda width: pl.BlockSpec((None, tm, width), lambda bi, i: (bi, i, 0))
    return pl.pallas_call(
        _ffn_kernel,
        grid=(bsz, s // tm),
        in_specs=[
            rows(ya.shape[2]), rows(yb.shape[2]), rows(d),
            const((d, d)), const((1, d)), const((1, d)),
            const((d, nu)), const((3, nu)), const((1, nu)), const((D_FF, d)), const((1, d)), const((1, d)),
        ],
        out_specs=rows(d),
        out_shape=jax.ShapeDtypeStruct((bsz, s, d), jnp.float32),
        scratch_shapes=[
            pltpu.VMEM((8, nu), jnp.float32),
            pltpu.VMEM((tm, D_FF), jnp.bfloat16),
        ],
        compiler_params=_params(("parallel", "arbitrary")),
        name="ffn",
    )(ya, yb, x, w_out, g1, b1, w_up, conv_w, conv_b, w_down, g2, b2)


def _layer(x, w_in, b_forget, rel_bias, w_out, ln1_g, ln1_b, w_up, conv_w, conv_b, w_down, ln2_g, ln2_b):
    b, s, d = x.shape
    bf = jnp.bfloat16
    fcol = 3 * D_A
    bcol = fcol + N_HEADS_A
    wqk = jnp.concatenate([w_in[:, 0:2 * D_A], w_in[:, bcol:bcol + 2 * D_B]], axis=1).astype(bf)
    wv = jnp.concatenate([w_in[:, 2 * D_A:fcol], w_in[:, bcol + 2 * D_B:]], axis=1).astype(bf)
    wf = jnp.pad(w_in[:, fcol:bcol], ((0, 0), (0, LANES - N_HEADS_A))).astype(bf)
    b_row = jnp.pad(b_forget, (0, LANES - N_HEADS_A)).reshape(1, LANES)

    qk, vt, kaug = _proj(x, wqk, wv, wf, b_row)
    ya = _fox(qk, kaug, vt)
    yb = _band(qk, vt, _band_bias(rel_bias))
    return _ffn(ya, yb, x, w_out.astype(bf), ln1_g.reshape(1, d), ln1_b.reshape(1, d),
                w_up.astype(bf), conv_w, conv_b.reshape(1, -1), w_down.astype(bf),
                ln2_g.reshape(1, d), ln2_b.reshape(1, d))


def kernel(x, w_in, b_forget, rel_bias, w_out, ln1_g, ln1_b, w_up, conv_w, conv_b, w_down, ln2_g, ln2_b):
    for l in range(DEPTH):
        x = _layer(x, w_in[l], b_forget[l], rel_bias[l], w_out[l], ln1_g[l], ln1_b[l], w_up[l],
                   conv_w[l], conv_b[l], w_down[l], ln2_g[l], ln2_b[l])
    return x
```

```python
import math

import jax
import jax.numpy as jnp
import numpy as np
from jax import lax
from jax.experimental import pallas as pl
from jax.experimental.pallas import tpu as pltpu

D_MODEL = 1024
HEAD_DIM = 64
N_HEADS_A = 8
N_HEADS_B = 8
D_A = N_HEADS_A * HEAD_DIM
D_B = N_HEADS_B * HEAD_DIM
CHUNK = 64
N_LEFT_CHUNKS = 8
REL_CLIP = 128
D_FF = 2816
LN_EPS = 1e-5
DEPTH = 1
DEEPNORM_ALPHA = (2.0 * DEPTH) ** 0.25

LANES = 128
PAIR = 2 * HEAD_DIM
N_PAIRS_A = D_A // PAIR
N_PAIRS_B = D_B // PAIR
ONES_ROWS = 16
NEG = -1e30
LOG2E = math.log2(math.e)
VMEM_LIMIT = 52 * 1024 * 1024

PROJ_TM = 512
FOX_TQ = 1024
FOX_TK = 512
FOX_STRIP = 256
BAND_HALO = N_LEFT_CHUNKS * CHUNK
BAND_TQ = 1024
BAND_G = 256
BAND_KW = BAND_G + BAND_HALO
BAND_SKEW = 2
FFN_TM = 512
FFN_CW = 256


def _params(sem, flags=None):
    return pltpu.CompilerParams(dimension_semantics=sem, vmem_limit_bytes=VMEM_LIMIT, flags=flags)


def _split3(v):
    h1 = v.astype(jnp.bfloat16).astype(jnp.float32)
    r1 = v - h1
    h2 = r1.astype(jnp.bfloat16).astype(jnp.float32)
    h3 = r1 - h2
    return h1, h2, h3


def _proj_kernel(x_ref, wqk_ref, wv_ref, wf_ref, b_ref, tri_ref, place_ref, qk_ref, vt_ref, ka_ref, carry_ref):
    i = pl.program_id(1)

    @pl.when(i == 0)
    def _():
        carry_ref[...] = jnp.zeros_like(carry_ref)

    xb = x_ref[...].astype(jnp.bfloat16)
    tm = xb.shape[0]
    z = jnp.dot(xb, wf_ref[...], preferred_element_type=jnp.float32) + b_ref[...]
    logf = -(jnp.maximum(-z, 0.0) + jnp.log1p(jnp.exp(-jnp.abs(z))))
    parts = jnp.concatenate(_split3(logf), axis=1).astype(jnp.bfloat16)

    qk = jnp.dot(xb, wqk_ref[...], preferred_element_type=jnp.float32)
    scale = HEAD_DIM ** -0.5 * LOG2E
    col = lax.broadcasted_iota(jnp.int32, (1, qk.shape[1]), 1)
    is_q = (col < D_A) | ((col >= 2 * D_A) & (col < 2 * D_A + D_B))
    qk_ref[...] = (qk * jnp.where(is_q, scale, 1.0)).astype(jnp.bfloat16)

    sums = jnp.dot(tri_ref[...], parts, preferred_element_type=jnp.float32)
    cum = carry_ref[0:1, :] + (sums[:, 0:LANES] + sums[:, LANES:2 * LANES] + sums[:, 2 * LANES:])
    carry_ref[...] = jnp.broadcast_to(cum[tm - 1:tm, :], carry_ref.shape)
    neg = cum * -LOG2E
    nparts = jnp.concatenate(_split3(neg), axis=1).astype(jnp.bfloat16)

    v = jnp.dot(xb, wv_ref[...], preferred_element_type=jnp.float32)
    vt_ref[...] = v.T.astype(jnp.bfloat16)

    ka = jnp.dot(nparts, place_ref[...], preferred_element_type=jnp.float32).astype(jnp.bfloat16)
    for hp in range(N_PAIRS_A):
        ka_ref[hp] = ka[:, hp * LANES:(hp + 1) * LANES]


def _gate_constants():
    tri = np.tril(np.ones((PROJ_TM, PROJ_TM), np.float32))
    place = np.zeros((3 * LANES, N_PAIRS_A * LANES), np.float32)
    for hp in range(N_PAIRS_A):
        for j in range(2):
            for t in range(3):
                place[t * LANES + 2 * hp + j, hp * LANES + 3 * j + t] = 1.0
    return jnp.asarray(tri, jnp.bfloat16), jnp.asarray(place, jnp.bfloat16)


def _proj(x, wqk, wv, wf, b_row):
    b, s, d = x.shape
    tm = PROJ_TM
    nqk = wqk.shape[1]
    nv = wv.shape[1]
    tri, place = _gate_constants()
    const = lambda arr: pl.BlockSpec(arr.shape, lambda bi, i: (0, 0))
    return pl.pallas_call(
        _proj_kernel,
        grid=(b, s // tm),
        in_specs=[
            pl.BlockSpec((None, tm, d), lambda bi, i: (bi, i, 0)),
            const(wqk), const(wv), const(wf), const(b_row), const(tri), const(place),
        ],
        out_specs=[
            pl.BlockSpec((None, tm, nqk), lambda bi, i: (bi, i, 0)),
            pl.BlockSpec((None, nv, tm), lambda bi, i: (bi, 0, i)),
            pl.BlockSpec((None, N_PAIRS_A, tm, LANES), lambda bi, i: (bi, 0, i, 0)),
        ],
        out_shape=[
            jax.ShapeDtypeStruct((b, s, nqk), jnp.bfloat16),
            jax.ShapeDtypeStruct((b, nv, s), jnp.bfloat16),
            jax.ShapeDtypeStruct((b, N_PAIRS_A, s, LANES), jnp.bfloat16),
        ],
        scratch_shapes=[pltpu.VMEM((8, LANES), jnp.float32)],
        compiler_params=_params(("parallel", "arbitrary")),
        name="proj",
    )(x, wqk, wv, wf, b_row, tri, place)


def _finalize(acc_e, acc_o):
    oe = acc_e[0:HEAD_DIM, :] / acc_e[HEAD_DIM:HEAD_DIM + 1, :]
    oo = acc_o[0:HEAD_DIM, :] / acc_o[HEAD_DIM:HEAD_DIM + 1, :]
    return jnp.concatenate([oe, oo], axis=0).T


def _vt_with_ones(vt_pair, h):
    ones_rows = jnp.ones((ONES_ROWS, vt_pair.shape[1]), jnp.bfloat16)
    return jnp.concatenate([vt_pair[h * HEAD_DIM:(h + 1) * HEAD_DIM, :], ones_rows], axis=0)


def _fox_kernel(q_ref, k_ref, ka_ref, vt_ref, o_ref, qe_ref, qo_ref, sa_ref, sb_ref, ma_ref, mb_ref,
                acc_e, acc_o, m_e, m_o):
    qi = pl.program_id(2)
    tq, tk = FOX_TQ, FOX_TK
    half = tq // 2
    row = lax.broadcasted_iota(jnp.int32, (LANES, tq), 0)
    qt = q_ref[...].astype(jnp.float32).T
    qe_ref[0:LANES, :] = jnp.where(row < HEAD_DIM, qt, 0.0).astype(jnp.bfloat16)
    qo_ref[0:LANES, :] = jnp.where(row < HEAD_DIM, 0.0, qt).astype(jnp.bfloat16)
    qe_ref[LANES:, :] = jnp.where(row < 3, 1.0, 0.0).astype(jnp.bfloat16)
    qo_ref[LANES:, :] = jnp.where((row >= 3) & (row < 6), 1.0, 0.0).astype(jnp.bfloat16)
    acc_e[...] = jnp.zeros_like(acc_e)
    acc_o[...] = jnp.zeros_like(acc_o)
    m_e[...] = jnp.full_like(m_e, NEG)
    m_o[...] = jnp.full_like(m_o, NEG)

    q_refs = (qe_ref, qo_ref)
    stats = ((acc_e, m_e), (acc_o, m_o))

    def key_operands(kb):
        ks = pl.multiple_of(kb * tk, tk)
        kcat = jnp.concatenate([k_ref[pl.ds(ks, tk), :], ka_ref[pl.ds(ks, tk), :]], axis=1)
        return kcat, vt_ref[:, pl.ds(ks, tk)]

    def score_strip(kcat, h, c, s_ref, mx_ref):
        cs = slice(c, c + FOX_STRIP)
        st = jnp.dot(kcat, q_refs[h][:, cs], preferred_element_type=jnp.float32)
        s_ref[h, :, cs] = st
        mx_ref[h, :, cs] = jnp.max(st, axis=0, keepdims=True)

    tri_r = lax.broadcasted_iota(jnp.int32, (FOX_STRIP, FOX_STRIP), 0)
    tri_c = lax.broadcasted_iota(jnp.int32, (FOX_STRIP, FOX_STRIP), 1)
    tri = jnp.where(tri_r <= tri_c, 0.0, NEG)

    def update_strip(vt_h, h, c, s_ref, mx_ref, mask_c0=None):
        acc, m = stats[h]
        cs = slice(c, c + FOX_STRIP)
        if mask_c0 is None:
            st = s_ref[h, :, cs]
            blk_max = mx_ref[h, :, cs]
        else:
            d = c - mask_c0
            diag = s_ref[h, d:d + FOX_STRIP, cs] + tri
            st = diag if d == 0 else jnp.concatenate([s_ref[h, 0:d, cs], diag], axis=0)
            vt_h = vt_h[:, 0:d + FOX_STRIP]
            blk_max = jnp.max(st, axis=0, keepdims=True)
        m_old = m[:, cs]
        m_new = jnp.maximum(m_old, blk_max)
        p = jnp.exp2(st - m_new).astype(jnp.bfloat16)
        pv = jnp.dot(vt_h, p, preferred_element_type=jnp.float32)
        acc[:, cs] = acc[:, cs] * jnp.exp2(m_old - m_new) + pv
        m[:, cs] = m_new

    def scores(kb, s_ref, mx_ref, c0=0, ncol=tq):
        kcat, _ = key_operands(kb)
        for h in range(2):
            for c in range(c0, c0 + ncol, FOX_STRIP):
                score_strip(kcat, h, c, s_ref, mx_ref)

    def step(kb_next, s_next, mx_next, kb_cur, s_cur, mx_cur):
        kcat, _ = key_operands(kb_next)
        _, vt_pair = key_operands(kb_cur)
        for h in range(2):
            vt_h = _vt_with_ones(vt_pair, h)
            for c in range(0, tq, FOX_STRIP):
                score_strip(kcat, h, c, s_next, mx_next)
                update_strip(vt_h, h, c, s_cur, mx_cur)

    kb0 = 2 * qi
    scores(0, sa_ref, ma_ref)

    def pair(kb):
        step(kb + 1, sb_ref, mb_ref, kb, sa_ref, ma_ref)
        step(kb + 2, sa_ref, ma_ref, kb + 1, sb_ref, mb_ref)

    def body(j, carry):
        pair(4 * j)
        pair(4 * j + 2)
        return carry

    lax.fori_loop(0, qi // 2, body, 0)

    @pl.when(qi % 2 == 1)
    def _():
        pair(2 * qi - 2)
    kcat1, vt_pair1 = key_operands(kb0 + 1)
    _, vt_pair0 = key_operands(kb0)
    for h in range(2):
        vt_h = _vt_with_ones(vt_pair0, h)
        for c in range(half, tq, FOX_STRIP):
            score_strip(kcat1, h, c, sb_ref, mb_ref)
            update_strip(vt_h, h, c, sa_ref, ma_ref)
        for c in range(0, half, FOX_STRIP):
            update_strip(vt_h, h, c, sa_ref, ma_ref, mask_c0=0)

    for h in range(2):
        vt_h = _vt_with_ones(vt_pair1, h)
        for c in range(half, tq, FOX_STRIP):
            update_strip(vt_h, h, c, sb_ref, mb_ref, mask_c0=half)
    o_ref[...] = _finalize(acc_e[...], acc_o[...]).astype(o_ref.dtype)


def _fox(qk, kaug, vt):
    b, s, _ = qk.shape
    tq, tk = FOX_TQ, FOX_TK
    kcol0 = D_A // LANES
    return pl.pallas_call(
        _fox_kernel,
        grid=(b, N_PAIRS_A, s // tq),
        in_specs=[
            pl.BlockSpec((None, tq, LANES), lambda bi, hp, qi: (bi, qi, hp)),
            pl.BlockSpec((None, s, LANES), lambda bi, hp, qi: (bi, 0, kcol0 + hp)),
            pl.BlockSpec((None, None, s, LANES), lambda bi, hp, qi: (bi, hp, 0, 0)),
            pl.BlockSpec((None, LANES, s), lambda bi, hp, qi: (bi, hp, 0)),
        ],
        out_specs=pl.BlockSpec((None, tq, LANES), lambda bi, hp, qi: (bi, qi, hp)),
        out_shape=jax.ShapeDtypeStruct((b, s, D_A), jnp.bfloat16),
        scratch_shapes=[
            pltpu.VMEM((2 * LANES, tq), jnp.bfloat16),
            pltpu.VMEM((2 * LANES, tq), jnp.bfloat16),
            pltpu.VMEM((2, tk, tq), jnp.float32),
            pltpu.VMEM((2, tk, tq), jnp.float32),
            pltpu.VMEM((2, 1, tq), jnp.float32),
            pltpu.VMEM((2, 1, tq), jnp.float32),
            pltpu.VMEM((HEAD_DIM + ONES_ROWS, tq), jnp.float32),
            pltpu.VMEM((HEAD_DIM + ONES_ROWS, tq), jnp.float32),
            pltpu.VMEM((1, tq), jnp.float32),
            pltpu.VMEM((1, tq), jnp.float32),
        ],
        compiler_params=_params(("parallel", "parallel", "arbitrary")),
        name="fox",
    )(qk, qk, kaug, vt)


def _band_kernel(q_ref, kp_ref, kc_ref, vtp_ref, vtc_ref, bias_ref, o_ref, kwin, vtwin, s_scr, p_scr):
    i = pl.program_id(2)
    tq, halo = BAND_TQ, BAND_HALO
    n_units = 2 * (tq // BAND_G)
    lane = lax.broadcasted_iota(jnp.int32, (halo, LANES), 1)
    kwin[0:halo, 0:LANES] = kp_ref[...]
    kwin[halo:, 0:LANES] = kc_ref[...]
    pen = jnp.where(i == 0, NEG, 0.0)
    kwin[0:halo, LANES:] = jnp.where(lane == 0, pen, 0.0).astype(jnp.bfloat16)
    kwin[halo:, LANES:] = jnp.zeros((tq, LANES), jnp.bfloat16)
    vtwin[:, 0:halo] = vtp_ref[...]
    vtwin[:, halo:] = vtc_ref[...]
    row_g = lax.broadcasted_iota(jnp.int32, (LANES, BAND_G), 0)
    ones_row = jnp.where(row_g == 0, 1.0, 0.0).astype(jnp.bfloat16)

    def scores(u):
        g, h = divmod(u, 2)
        qt = q_ref[g * BAND_G:(g + 1) * BAND_G, :].astype(jnp.float32).T
        own = (row_g < HEAD_DIM) if h == 0 else (row_g >= HEAD_DIM)
        qcat = jnp.concatenate([jnp.where(own, qt, 0.0).astype(jnp.bfloat16), ones_row], axis=0)
        kslab = kwin[g * BAND_G:g * BAND_G + BAND_KW, :]
        s_scr[u] = jnp.dot(kslab, qcat, preferred_element_type=jnp.float32)

    def softmax(u):
        st = s_scr[u] + bias_ref[u % 2]
        m = jnp.max(st, axis=0, keepdims=True)
        p_scr[u] = jnp.exp2(st - m).astype(jnp.bfloat16)

    accs = {}

    def pv(u):
        g, h = divmod(u, 2)
        vslab = vtwin[:, g * BAND_G:g * BAND_G + BAND_KW]
        accs[h] = jnp.dot(_vt_with_ones(vslab, h), p_scr[u], preferred_element_type=jnp.float32)
        if h == 1:
            o_ref[g * BAND_G:(g + 1) * BAND_G, :] = _finalize(accs[0], accs[1]).astype(o_ref.dtype)

    for t in range(n_units + 2 * BAND_SKEW):
        if t < n_units:
            scores(t)
        if BAND_SKEW <= t < n_units + BAND_SKEW:
            softmax(t - BAND_SKEW)
        if t >= 2 * BAND_SKEW:
            pv(t - 2 * BAND_SKEW)


def _band(qk, vt, bias_t):
    b, s, _ = qk.shape
    tq, halo = BAND_TQ, BAND_HALO
    r = tq // halo
    qcol0 = 2 * D_A // LANES
    kcol0 = (2 * D_A + D_B) // LANES
    vrow0 = D_A // PAIR
    prev = lambda i: jnp.maximum(i * r - 1, 0)
    n_units = 2 * (tq // BAND_G)
    return pl.pallas_call(
        _band_kernel,
        grid=(b, N_PAIRS_B, s // tq),
        in_specs=[
            pl.BlockSpec((None, tq, LANES), lambda bi, hp, i: (bi, i, qcol0 + hp)),
            pl.BlockSpec((None, halo, LANES), lambda bi, hp, i: (bi, prev(i), kcol0 + hp)),
            pl.BlockSpec((None, tq, LANES), lambda bi, hp, i: (bi, i, kcol0 + hp)),
            pl.BlockSpec((None, PAIR, halo), lambda bi, hp, i: (bi, vrow0 + hp, prev(i))),
            pl.BlockSpec((None, PAIR, tq), lambda bi, hp, i: (bi, vrow0 + hp, i)),
            pl.BlockSpec((2, BAND_KW, BAND_G), lambda bi, hp, i: (hp, 0, 0)),
        ],
        out_specs=pl.BlockSpec((None, tq, LANES), lambda bi, hp, i: (bi, i, hp)),
        out_shape=jax.ShapeDtypeStruct((b, s, D_B), jnp.bfloat16),
        scratch_shapes=[
            pltpu.VMEM((halo + tq, 2 * LANES), jnp.bfloat16),
            pltpu.VMEM((PAIR, halo + tq), jnp.bfloat16),
            pltpu.VMEM((n_units, BAND_KW, BAND_G), jnp.float32),
            pltpu.VMEM((n_units, BAND_KW, BAND_G), jnp.bfloat16),
        ],
        compiler_params=_params(("parallel", "parallel", "arbitrary")),
        name="band",
    )(qk, qk, qk, vt, vt, bias_t)


BIAS_W = 1024


def _bias_kernel(g_ref, o_ref):
    row = jnp.broadcast_to(g_ref[...], (BAND_KW, BIAS_W))
    toe = pltpu.roll(row, 0, 1, stride=1, stride_axis=0)[:, 0:BAND_G]
    kc = lax.broadcasted_iota(jnp.int32, (BAND_KW, BAND_G), 0) // CHUNK
    qc = lax.broadcasted_iota(jnp.int32, (BAND_KW, BAND_G), 1) // CHUNK
    inband = (kc >= qc) & (kc <= qc + N_LEFT_CHUNKS)
    o_ref[...] = jnp.where(inband, toe * LOG2E, NEG)


def _band_bias(rel_bias):
    h = rel_bias.shape[0]
    j = np.arange(BIAS_W)
    off = np.where(j < BAND_G, j, j - BIAS_W)
    idx = np.clip(N_LEFT_CHUNKS * CHUNK + off, -REL_CLIP, REL_CLIP) + REL_CLIP
    g_ext = rel_bias.astype(jnp.float32)[:, idx].reshape(h, 1, BIAS_W)
    return pl.pallas_call(
        _bias_kernel,
        grid=(h,),
        in_specs=[pl.BlockSpec((None, 1, BIAS_W), lambda i: (i, 0, 0))],
        out_specs=pl.BlockSpec((None, BAND_KW, BAND_G), lambda i: (i, 0, 0)),
        out_shape=jax.ShapeDtypeStruct((h, BAND_KW, BAND_G), jnp.float32),
        compiler_params=_params(("parallel",)),
        name="bias",
    )(g_ext)


def _layer_norm(z, g, b):
    mu = jnp.mean(z, axis=-1, keepdims=True)
    zc = z - mu
    var = jnp.mean(zc * zc, axis=-1, keepdims=True)
    return zc * lax.rsqrt(var + LN_EPS) * g + b


def _gelu_tanh(x):
    return 0.5 * x * (1.0 + jnp.tanh(math.sqrt(2.0 / math.pi) * (x + 0.044715 * (x * x * x))))


def _causal_conv(u, carry, w, bias):
    tm = u.shape[0]
    r8 = lax.broadcasted_iota(jnp.int32, (8, u.shape[1]), 0)
    outs = w[2:3, :] * u + bias
    for shift, j in ((1, 1), (2, 0)):
        ur = pltpu.roll(u, shift, 0)
        cr = pltpu.roll(carry, shift, 0)
        head = jnp.where(r8 < shift, cr, ur[0:8, :])
        us = jnp.concatenate([head, ur[8:tm, :]], axis=0)
        outs = outs + w[j:j + 1, :] * us
    return outs


def _ffn_kernel(ya_ref, yb_ref, x_ref, wout_ref, g1_ref, b1_ref, wup_ref, cw_ref, cb_ref, wdn_ref, g2_ref, b2_ref,
                o_ref, carry_ref, h_ref):
    i = pl.program_id(1)

    @pl.when(i == 0)
    def _():
        carry_ref[...] = jnp.zeros_like(carry_ref)

    y = jnp.concatenate([ya_ref[...], yb_ref[...]], axis=1)
    mix = jnp.dot(y, wout_ref[...], preferred_element_type=jnp.float32)
    x1 = _layer_norm(DEEPNORM_ALPHA * x_ref[...] + mix, g1_ref[...], b1_ref[...])
    xb = x1.astype(jnp.bfloat16)
    tm = x1.shape[0]
    for c in range(D_FF // FFN_CW):
        halves = []
        for off in (c * FFN_CW, D_FF + c * FFN_CW):
            sl = slice(off, off + FFN_CW)
            u = jnp.dot(xb, wup_ref[:, sl], preferred_element_type=jnp.float32)
            halves.append(_causal_conv(u, carry_ref[:, sl], cw_ref[:, sl], cb_ref[:, sl]))
            carry_ref[:, sl] = u[tm - 8:tm, :]
        h_ref[:, c * FFN_CW:(c + 1) * FFN_CW] = (halves[0] * _gelu_tanh(halves[1])).astype(jnp.bfloat16)
    ffn = jnp.dot(h_ref[...], wdn_ref[...], preferred_element_type=jnp.float32)
    o_ref[...] = _layer_norm(DEEPNORM_ALPHA * x1 + ffn, g2_ref[...], b2_ref[...])


def _ffn(ya, yb, x, w_out, g1, b1, w_up, conv_w, conv_b, w_down, g2, b2):
    bsz, s, d = x.shape
    tm = FFN_TM
    nu = w_up.shape[1]
    const = lambda shape: pl.BlockSpec(shape, lambda bi, i: (0, 0), pipeline_mode=pl.Buffered(1))
    rows = lambda width: pl.BlockSpec((None, tm, width), lambda bi, i: (bi, i, 0))
    return pl.pallas_call(
        _ffn_kernel,
        grid=(bsz, s // tm),
        in_specs=[
            rows(ya.shape[2]), rows(yb.shape[2]), rows(d),
            const((d, d)), const((1, d)), const((1, d)),
            const((d, nu)), const((3, nu)), const((1, nu)), const((D_FF, d)), const((1, d)), const((1, d)),
        ],
        out_specs=rows(d),
        out_shape=jax.ShapeDtypeStruct((bsz, s, d), jnp.float32),
        scratch_shapes=[
            pltpu.VMEM((8, nu), jnp.float32),
            pltpu.VMEM((tm, D_FF), jnp.bfloat16),
        ],
        compiler_params=_params(("parallel", "arbitrary")),
        name="ffn",
    )(ya, yb, x, w_out, g1, b1, w_up, conv_w, conv_b, w_down, g2, b2)


def _layer(x, w_in, b_forget, rel_bias, w_out, ln1_g, ln1_b, w_up, conv_w, conv_b, w_down, ln2_g, ln2_b):
    b, s, d = x.shape
    bf = jnp.bfloat16
    fcol = 3 * D_A
    bcol = fcol + N_HEADS_A
    wqk = jnp.concatenate([w_in[:, 0:2 * D_A], w_in[:, bcol:bcol + 2 * D_B]], axis=1).astype(bf)
    wv = jnp.concatenate([w_in[:, 2 * D_A:fcol], w_in[:, bcol + 2 * D_B:]], axis=1).astype(bf)
    wf = jnp.pad(w_in[:, fcol:bcol], ((0, 0), (0, LANES - N_HEADS_A))).astype(bf)
    b_row = jnp.pad(b_forget, (0, LANES - N_HEADS_A)).reshape(1, LANES)

    qk, vt, kaug = _proj(x, wqk, wv, wf, b_row)
    ya = _fox(qk, kaug, vt)
    yb = _band(qk, vt, _band_bias(rel_bias))
    return _ffn(ya, yb, x, w_out.astype(bf), ln1_g.reshape(1, d), ln1_b.reshape(1, d),
                w_up.astype(bf), conv_w, conv_b.reshape(1, -1), w_down.astype(bf),
                ln2_g.reshape(1, d), ln2_b.reshape(1, d))


def kernel(x, w_in, b_forget, rel_bias, w_out, ln1_g, ln1_b, w_up, conv_w, conv_b, w_down, ln2_g, ln2_b):
    for l in range(DEPTH):
        x = _layer(x, w_in[l], b_forget[l], rel_bias[l], w_out[l], ln1_g[l], ln1_b[l], w_up[l],
                   conv_w[l], conv_b[l], w_down[l], ln2_g[l], ln2_b[l])
    return x
```

```python
import math

import jax
import jax.numpy as jnp
import numpy as np
from jax import lax
from jax.experimental import pallas as pl
from jax.experimental.pallas import tpu as pltpu

D_MODEL = 1024
HEAD_DIM = 64
N_HEADS_A = 8
N_HEADS_B = 8
D_A = N_HEADS_A * HEAD_DIM
D_B = N_HEADS_B * HEAD_DIM
CHUNK = 64
N_LEFT_CHUNKS = 8
REL_CLIP = 128
D_FF = 2816
LN_EPS = 1e-5
DEPTH = 1
DEEPNORM_ALPHA = (2.0 * DEPTH) ** 0.25

LANES = 128
PAIR = 2 * HEAD_DIM
N_PAIRS_A = D_A // PAIR
N_PAIRS_B = D_B // PAIR
ONES_ROWS = 16
NEG = -1e30
LOG2E = math.log2(math.e)
VMEM_LIMIT = 52 * 1024 * 1024

PROJ_TM = 512
FOX_TQ = 2048
FOX_TK = 512
FOX_STRIP = 256
BAND_HALO = N_LEFT_CHUNKS * CHUNK
BAND_TQ = 1024
BAND_G = 256
BAND_KW = BAND_G + BAND_HALO
BAND_SKEW = 2
FFN_TM = 512
FFN_CW = 256


def _params(sem, flags=None):
    return pltpu.CompilerParams(dimension_semantics=sem, vmem_limit_bytes=VMEM_LIMIT, flags=flags)


def _split3(v):
    h1 = v.astype(jnp.bfloat16).astype(jnp.float32)
    r1 = v - h1
    h2 = r1.astype(jnp.bfloat16).astype(jnp.float32)
    h3 = r1 - h2
    return h1, h2, h3


def _proj_kernel(x_ref, wqk_ref, wv_ref, wf_ref, b_ref, tri_ref, place_ref, qk_ref, vt_ref, ka_ref, carry_ref):
    i = pl.program_id(1)

    @pl.when(i == 0)
    def _():
        carry_ref[...] = jnp.zeros_like(carry_ref)

    xb = x_ref[...].astype(jnp.bfloat16)
    tm = xb.shape[0]
    z = jnp.dot(xb, wf_ref[...], preferred_element_type=jnp.float32) + b_ref[...]
    logf = -(jnp.maximum(-z, 0.0) + jnp.log1p(jnp.exp(-jnp.abs(z))))
    parts = jnp.concatenate(_split3(logf), axis=1).astype(jnp.bfloat16)

    qk = jnp.dot(xb, wqk_ref[...], preferred_element_type=jnp.float32)
    scale = HEAD_DIM ** -0.5 * LOG2E
    col = lax.broadcasted_iota(jnp.int32, (1, qk.shape[1]), 1)
    is_q = (col < D_A) | ((col >= 2 * D_A) & (col < 2 * D_A + D_B))
    qk_ref[...] = (qk * jnp.where(is_q, scale, 1.0)).astype(jnp.bfloat16)

    sums = jnp.dot(tri_ref[...], parts, preferred_element_type=jnp.float32)
    cum = carry_ref[0:1, :] + (sums[:, 0:LANES] + sums[:, LANES:2 * LANES] + sums[:, 2 * LANES:])
    carry_ref[...] = jnp.broadcast_to(cum[tm - 1:tm, :], carry_ref.shape)
    neg = cum * -LOG2E
    nparts = jnp.concatenate(_split3(neg), axis=1).astype(jnp.bfloat16)

    v = jnp.dot(xb, wv_ref[...], preferred_element_type=jnp.float32)
    vt_ref[...] = v.T.astype(jnp.bfloat16)

    ka = jnp.dot(nparts, place_ref[...], preferred_element_type=jnp.float32).astype(jnp.bfloat16)
    for hp in range(N_PAIRS_A):
        ka_ref[hp] = ka[:, hp * LANES:(hp + 1) * LANES]


def _gate_constants():
    tri = np.tril(np.ones((PROJ_TM, PROJ_TM), np.float32))
    place = np.zeros((3 * LANES, N_PAIRS_A * LANES), np.float32)
    for hp in range(N_PAIRS_A):
        for j in range(2):
            for t in range(3):
                place[t * LANES + 2 * hp + j, hp * LANES + 3 * j + t] = 1.0
    return jnp.asarray(tri, jnp.bfloat16), jnp.asarray(place, jnp.bfloat16)


def _proj(x, wqk, wv, wf, b_row):
    b, s, d = x.shape
    tm = PROJ_TM
    nqk = wqk.shape[1]
    nv = wv.shape[1]
    tri, place = _gate_constants()
    const = lambda arr: pl.BlockSpec(arr.shape, lambda bi, i: (0, 0))
    return pl.pallas_call(
        _proj_kernel,
        grid=(b, s // tm),
        in_specs=[
            pl.BlockSpec((None, tm, d), lambda bi, i: (bi, i, 0)),
            const(wqk), const(wv), const(wf), const(b_row), const(tri), const(place),
        ],
        out_specs=[
            pl.BlockSpec((None, tm, nqk), lambda bi, i: (bi, i, 0)),
            pl.BlockSpec((None, nv, tm), lambda bi, i: (bi, 0, i)),
            pl.BlockSpec((None, N_PAIRS_A, tm, LANES), lambda bi, i: (bi, 0, i, 0)),
        ],
        out_shape=[
            jax.ShapeDtypeStruct((b, s, nqk), jnp.bfloat16),
            jax.ShapeDtypeStruct((b, nv, s), jnp.bfloat16),
            jax.ShapeDtypeStruct((b, N_PAIRS_A, s, LANES), jnp.bfloat16),
        ],
        scratch_shapes=[pltpu.VMEM((8, LANES), jnp.float32)],
        compiler_params=_params(("parallel", "arbitrary")),
        name="proj",
    )(x, wqk, wv, wf, b_row, tri, place)


def _finalize(acc_e, acc_o):
    oe = acc_e[0:HEAD_DIM, :] / acc_e[HEAD_DIM:HEAD_DIM + 1, :]
    oo = acc_o[0:HEAD_DIM, :] / acc_o[HEAD_DIM:HEAD_DIM + 1, :]
    return jnp.concatenate([oe, oo], axis=0).T


def _vt_with_ones(vt_pair, h):
    ones_rows = jnp.ones((ONES_ROWS, vt_pair.shape[1]), jnp.bfloat16)
    return jnp.concatenate([vt_pair[h * HEAD_DIM:(h + 1) * HEAD_DIM, :], ones_rows], axis=0)


def _fox_kernel(q_ref, k_ref, ka_ref, vt_ref, o_ref, qe_ref, qo_ref, sa_ref, sb_ref, ma_ref, mb_ref,
                acc_e, acc_o, m_e, m_o):
    qi = pl.program_id(2)
    tq, tk = FOX_TQ, FOX_TK
    row = lax.broadcasted_iota(jnp.int32, (LANES, tq), 0)
    qt = q_ref[...].astype(jnp.float32).T
    qe_ref[0:LANES, :] = jnp.where(row < HEAD_DIM, qt, 0.0).astype(jnp.bfloat16)
    qo_ref[0:LANES, :] = jnp.where(row < HEAD_DIM, 0.0, qt).astype(jnp.bfloat16)
    qe_ref[LANES:, :] = jnp.where(row < 3, 1.0, 0.0).astype(jnp.bfloat16)
    qo_ref[LANES:, :] = jnp.where((row >= 3) & (row < 6), 1.0, 0.0).astype(jnp.bfloat16)
    acc_e[...] = jnp.zeros_like(acc_e)
    acc_o[...] = jnp.zeros_like(acc_o)
    m_e[...] = jnp.full_like(m_e, NEG)
    m_o[...] = jnp.full_like(m_o, NEG)

    q_refs = (qe_ref, qo_ref)
    stats = ((acc_e, m_e), (acc_o, m_o))

    def key_operands(kb):
        ks = pl.multiple_of(kb * tk, tk)
        kcat = jnp.concatenate([k_ref[pl.ds(ks, tk), :], ka_ref[pl.ds(ks, tk), :]], axis=1)
        return kcat, vt_ref[:, pl.ds(ks, tk)]

    def score_strip(kcat, h, c, s_ref, mx_ref):
        cs = slice(c, c + FOX_STRIP)
        st = jnp.dot(kcat, q_refs[h][:, cs], preferred_element_type=jnp.float32)
        s_ref[h, :, cs] = st
        mx_ref[h, :, cs] = jnp.max(st, axis=0, keepdims=True)

    tri_r = lax.broadcasted_iota(jnp.int32, (FOX_STRIP, FOX_STRIP), 0)
    tri_c = lax.broadcasted_iota(jnp.int32, (FOX_STRIP, FOX_STRIP), 1)
    tri = jnp.where(tri_r <= tri_c, 0.0, NEG)

    def update_strip(vt_h, h, c, s_ref, mx_ref, mask_c0=None):
        acc, m = stats[h]
        cs = slice(c, c + FOX_STRIP)
        if mask_c0 is None:
            st = s_ref[h, :, cs]
            blk_max = mx_ref[h, :, cs]
        else:
            d = c - mask_c0
            diag = s_ref[h, d:d + FOX_STRIP, cs] + tri
            st = diag if d == 0 else jnp.concatenate([s_ref[h, 0:d, cs], diag], axis=0)
            vt_h = vt_h[:, 0:d + FOX_STRIP]
            blk_max = jnp.max(st, axis=0, keepdims=True)
        m_old = m[:, cs]
        m_new = jnp.maximum(m_old, blk_max)
        p = jnp.exp2(st - m_new).astype(jnp.bfloat16)
        pv = jnp.dot(vt_h, p, preferred_element_type=jnp.float32)
        acc[:, cs] = acc[:, cs] * jnp.exp2(m_old - m_new) + pv
        m[:, cs] = m_new

    def scores(kb, s_ref, mx_ref, c0=0, ncol=tq):
        kcat, _ = key_operands(kb)
        for h in range(2):
            for c in range(c0, c0 + ncol, FOX_STRIP):
                score_strip(kcat, h, c, s_ref, mx_ref)

    def step(kb_next, s_next, mx_next, kb_cur, s_cur, mx_cur):
        kcat, _ = key_operands(kb_next)
        _, vt_pair = key_operands(kb_cur)
        for h in range(2):
            vt_h = _vt_with_ones(vt_pair, h)
            for c in range(0, tq, FOX_STRIP):
                score_strip(kcat, h, c, s_next, mx_next)
                update_strip(vt_h, h, c, s_cur, mx_cur)

    n_own = tq // tk
    kb0 = n_own * qi
    scores(0, sa_ref, ma_ref)

    def pair(kb):
        step(kb + 1, sb_ref, mb_ref, kb, sa_ref, ma_ref)
        step(kb + 2, sa_ref, ma_ref, kb + 1, sb_ref, mb_ref)

    def body(j, carry):
        pair(4 * j)
        pair(4 * j + 2)
        return carry

    n_pairs = kb0 // 2
    lax.fori_loop(0, n_pairs // 2, body, 0)
    if n_own % 4:
        @pl.when(n_pairs % 2 == 1)
        def _():
            pair(kb0 - 2)
    bufs = ((sa_ref, ma_ref), (sb_ref, mb_ref))
    for j in range(n_own):
        s_cur, mx_cur = bufs[j % 2]
        s_nxt, mx_nxt = bufs[(j + 1) % 2]
        _, vt_pair = key_operands(kb0 + j)
        kcat_nxt = key_operands(kb0 + j + 1)[0] if j + 1 < n_own else None
        for h in range(2):
            vt_h = _vt_with_ones(vt_pair, h)
            for c in range((j + 1) * tk, tq, FOX_STRIP):
                score_strip(kcat_nxt, h, c, s_nxt, mx_nxt)
                update_strip(vt_h, h, c, s_cur, mx_cur)
            for c in range(j * tk, (j + 1) * tk, FOX_STRIP):
                update_strip(vt_h, h, c, s_cur, mx_cur, mask_c0=j * tk)
    o_ref[...] = _finalize(acc_e[...], acc_o[...]).astype(o_ref.dtype)


def _fox(qk, kaug, vt):
    b, s, _ = qk.shape
    tq, tk = FOX_TQ, FOX_TK
    kcol0 = D_A // LANES
    return pl.pallas_call(
        _fox_kernel,
        grid=(b, N_PAIRS_A, s // tq),
        in_specs=[
            pl.BlockSpec((None, tq, LANES), lambda bi, hp, qi: (bi, qi, hp)),
            pl.BlockSpec((None, s, LANES), lambda bi, hp, qi: (bi, 0, kcol0 + hp)),
            pl.BlockSpec((None, None, s, LANES), lambda bi, hp, qi: (bi, hp, 0, 0)),
            pl.BlockSpec((None, LANES, s), lambda bi, hp, qi: (bi, hp, 0)),
        ],
        out_specs=pl.BlockSpec((None, tq, LANES), lambda bi, hp, qi: (bi, qi, hp)),
        out_shape=jax.ShapeDtypeStruct((b, s, D_A), jnp.bfloat16),
        scratch_shapes=[
            pltpu.VMEM((2 * LANES, tq), jnp.bfloat16),
            pltpu.VMEM((2 * LANES, tq), jnp.bfloat16),
            pltpu.VMEM((2, tk, tq), jnp.float32),
            pltpu.VMEM((2, tk, tq), jnp.float32),
            pltpu.VMEM((2, 1, tq), jnp.float32),
            pltpu.VMEM((2, 1, tq), jnp.float32),
            pltpu.VMEM((HEAD_DIM + ONES_ROWS, tq), jnp.float32),
            pltpu.VMEM((HEAD_DIM + ONES_ROWS, tq), jnp.float32),
            pltpu.VMEM((1, tq), jnp.float32),
            pltpu.VMEM((1, tq), jnp.float32),
        ],
        compiler_params=_params(("parallel", "parallel", "arbitrary")),
        name="fox",
    )(qk, qk, kaug, vt)


def _band_kernel(q_ref, kp_ref, kc_ref, vtp_ref, vtc_ref, bias_ref, o_ref, kwin, vtwin, s_scr, p_scr):
    i = pl.program_id(2)
    tq, halo = BAND_TQ, BAND_HALO
    n_units = 2 * (tq // BAND_G)
    lane = lax.broadcasted_iota(jnp.int32, (halo, LANES), 1)
    kwin[0:halo, 0:LANES] = kp_ref[...]
    kwin[halo:, 0:LANES] = kc_ref[...]
    pen = jnp.where(i == 0, NEG, 0.0)
    kwin[0:halo, LANES:] = jnp.where(lane == 0, pen, 0.0).astype(jnp.bfloat16)
    kwin[halo:, LANES:] = jnp.zeros((tq, LANES), jnp.bfloat16)
    vtwin[:, 0:halo] = vtp_ref[...]
    vtwin[:, halo:] = vtc_ref[...]
    row_g = lax.broadcasted_iota(jnp.int32, (LANES, BAND_G), 0)
    ones_row = jnp.where(row_g == 0, 1.0, 0.0).astype(jnp.bfloat16)

    def scores(u):
        g, h = divmod(u, 2)
        qt = q_ref[g * BAND_G:(g + 1) * BAND_G, :].astype(jnp.float32).T
        own = (row_g < HEAD_DIM) if h == 0 else (row_g >= HEAD_DIM)
        qcat = jnp.concatenate([jnp.where(own, qt, 0.0).astype(jnp.bfloat16), ones_row], axis=0)
        kslab = kwin[g * BAND_G:g * BAND_G + BAND_KW, :]
        s_scr[u] = jnp.dot(kslab, qcat, preferred_element_type=jnp.float32)

    def softmax(u):
        st = s_scr[u] + bias_ref[u % 2]
        m = jnp.max(st, axis=0, keepdims=True)
        p_scr[u] = jnp.exp2(st - m).astype(jnp.bfloat16)

    accs = {}

    def pv(u):
        g, h = divmod(u, 2)
        vslab = vtwin[:, g * BAND_G:g * BAND_G + BAND_KW]
        accs[h] = jnp.dot(_vt_with_ones(vslab, h), p_scr[u], preferred_element_type=jnp.float32)
        if h == 1:
            o_ref[g * BAND_G:(g + 1) * BAND_G, :] = _finalize(accs[0], accs[1]).astype(o_ref.dtype)

    for t in range(n_units + 2 * BAND_SKEW):
        if t < n_units:
            scores(t)
        if BAND_SKEW <= t < n_units + BAND_SKEW:
            softmax(t - BAND_SKEW)
        if t >= 2 * BAND_SKEW:
            pv(t - 2 * BAND_SKEW)


def _band(qk, vt, bias_t):
    b, s, _ = qk.shape
    tq, halo = BAND_TQ, BAND_HALO
    r = tq // halo
    qcol0 = 2 * D_A // LANES
    kcol0 = (2 * D_A + D_B) // LANES
    vrow0 = D_A // PAIR
    prev = lambda i: jnp.maximum(i * r - 1, 0)
    n_units = 2 * (tq // BAND_G)
    return pl.pallas_call(
        _band_kernel,
        grid=(b, N_PAIRS_B, s // tq),
        in_specs=[
            pl.BlockSpec((None, tq, LANES), lambda bi, hp, i: (bi, i, qcol0 + hp)),
            pl.BlockSpec((None, halo, LANES), lambda bi, hp, i: (bi, prev(i), kcol0 + hp)),
            pl.BlockSpec((None, tq, LANES), lambda bi, hp, i: (bi, i, kcol0 + hp)),
            pl.BlockSpec((None, PAIR, halo), lambda bi, hp, i: (bi, vrow0 + hp, prev(i))),
            pl.BlockSpec((None, PAIR, tq), lambda bi, hp, i: (bi, vrow0 + hp, i)),
            pl.BlockSpec((2, BAND_KW, BAND_G), lambda bi, hp, i: (hp, 0, 0)),
        ],
        out_specs=pl.BlockSpec((None, tq, LANES), lambda bi, hp, i: (bi, i, hp)),
        out_shape=jax.ShapeDtypeStruct((b, s, D_B), jnp.bfloat16),
        scratch_shapes=[
            pltpu.VMEM((halo + tq, 2 * LANES), jnp.bfloat16),
            pltpu.VMEM((PAIR, halo + tq), jnp.bfloat16),
            pltpu.VMEM((n_units, BAND_KW, BAND_G), jnp.float32),
            pltpu.VMEM((n_units, BAND_KW, BAND_G), jnp.bfloat16),
        ],
        compiler_params=_params(("parallel", "parallel", "arbitrary")),
        name="band",
    )(qk, qk, qk, vt, vt, bias_t)


BIAS_W = 1024


def _bias_kernel(g_ref, o_ref):
    row = jnp.broadcast_to(g_ref[...], (BAND_KW, BIAS_W))
    toe = pltpu.roll(row, 0, 1, stride=1, stride_axis=0)[:, 0:BAND_G]
    kc = lax.broadcasted_iota(jnp.int32, (BAND_KW, BAND_G), 0) // CHUNK
    qc = lax.broadcasted_iota(jnp.int32, (BAND_KW, BAND_G), 1) // CHUNK
    inband = (kc >= qc) & (kc <= qc + N_LEFT_CHUNKS)
    o_ref[...] = jnp.where(inband, toe * LOG2E, NEG)


def _band_bias(rel_bias):
    h = rel_bias.shape[0]
    j = np.arange(BIAS_W)
    off = np.where(j < BAND_G, j, j - BIAS_W)
    idx = np.clip(N_LEFT_CHUNKS * CHUNK + off, -REL_CLIP, REL_CLIP) + REL_CLIP
    g_ext = rel_bias.astype(jnp.float32)[:, idx].reshape(h, 1, BIAS_W)
    return pl.pallas_call(
        _bias_kernel,
        grid=(h,),
        in_specs=[pl.BlockSpec((None, 1, BIAS_W), lambda i: (i, 0, 0))],
        out_specs=pl.BlockSpec((None, BAND_KW, BAND_G), lambda i: (i, 0, 0)),
        out_shape=jax.ShapeDtypeStruct((h, BAND_KW, BAND_G), jnp.float32),
        compiler_params=_params(("parallel",)),
        name="bias",
    )(g_ext)


def _layer_norm(z, g, b):
    mu = jnp.mean(z, axis=-1, keepdims=True)
    zc = z - mu
    var = jnp.mean(zc * zc, axis=-1, keepdims=True)
    return zc * lax.rsqrt(var + LN_EPS) * g + b


def _gelu_tanh(x):
    return 0.5 * x * (1.0 + jnp.tanh(math.sqrt(2.0 / math.pi) * (x + 0.044715 * (x * x * x))))


def _causal_conv(u, carry, w, bias):
    tm = u.shape[0]
    r8 = lax.broadcasted_iota(jnp.int32, (8, u.shape[1]), 0)
    outs = w[2:3, :] * u + bias
    for shift, j in ((1, 1), (2, 0)):
        ur = pltpu.roll(u, shift, 0)
        cr = pltpu.roll(carry, shift, 0)
        head = jnp.where(r8 < shift, cr, ur[0:8, :])
        us = jnp.concatenate([head, ur[8:tm, :]], axis=0)
        outs = outs + w[j:j + 1, :] * us
    return outs


def _ffn_kernel(ya_ref, yb_ref, x_ref, wout_ref, g1_ref, b1_ref, wup_ref, cw_ref, cb_ref, wdn_ref, g2_ref, b2_ref,
                o_ref, carry_ref, h_ref):
    i = pl.program_id(1)

    @pl.when(i == 0)
    def _():
        carry_ref[...] = jnp.zeros_like(carry_ref)

    y = jnp.concatenate([ya_ref[...], yb_ref[...]], axis=1)
    mix = jnp.dot(y, wout_ref[...], preferred_element_type=jnp.float32)
    x1 = _layer_norm(DEEPNORM_ALPHA * x_ref[...] + mix, g1_ref[...], b1_ref[...])
    xb = x1.astype(jnp.bfloat16)
    tm = x1.shape[0]
    for c in range(D_FF // FFN_CW):
        halves = []
        for off in (c * FFN_CW, D_FF + c * FFN_CW):
            sl = slice(off, off + FFN_CW)
            u = jnp.dot(xb, wup_ref[:, sl], preferred_element_type=jnp.float32)
            halves.append(_causal_conv(u, carry_ref[:, sl], cw_ref[:, sl], cb_ref[:, sl]))
            carry_ref[:, sl] = u[tm - 8:tm, :]
        h_ref[:, c * FFN_CW:(c + 1) * FFN_CW] = (halves[0] * _gelu_tanh(halves[1])).astype(jnp.bfloat16)
    ffn = jnp.dot(h_ref[...], wdn_ref[...], preferred_element_type=jnp.float32)
    o_ref[...] = _layer_norm(DEEPNORM_ALPHA * x1 + ffn, g2_ref[...], b2_ref[...])


def _ffn(ya, yb, x, w_out, g1, b1, w_up, conv_w, conv_b, w_down, g2, b2):
    bsz, s, d = x.shape
    tm = FFN_TM
    nu = w_up.shape[1]
    const = lambda shape: pl.BlockSpec(shape, lambda bi, i: (0, 0), pipeline_mode=pl.Buffered(1))
    rows = lambda width: pl.BlockSpec((None, tm, width), lambda bi, i: (bi, i, 0))
    return pl.pallas_call(
        _ffn_kernel,
        grid=(bsz, s // tm),
        in_specs=[
            rows(ya.shape[2]), rows(yb.shape[2]), rows(d),
            const((d, d)), const((1, d)), const((1, d)),
            const((d, nu)), const((3, nu)), const((1, nu)), const((D_FF, d)), const((1, d)), const((1, d)),
        ],
        out_specs=rows(d),
        out_shape=jax.ShapeDtypeStruct((bsz, s, d), jnp.float32),
        scratch_shapes=[
            pltpu.VMEM((8, nu), jnp.float32),
            pltpu.VMEM((tm, D_FF), jnp.bfloat16),
        ],
        compiler_params=_params(("parallel", "arbitrary")),
        name="ffn",
    )(ya, yb, x, w_out, g1, b1, w_up, conv_w, conv_b, w_down, g2, b2)


def _layer(x, w_in, b_forget, rel_bias, w_out, ln1_g, ln1_b, w_up, conv_w, conv_b, w_down, ln2_g, ln2_b):
    b, s, d = x.shape
    bf = jnp.bfloat16
    fcol = 3 * D_A
    bcol = fcol + N_HEADS_A
    wqk = jnp.concatenate([w_in[:, 0:2 * D_A], w_in[:, bcol:bcol + 2 * D_B]], axis=1).astype(bf)
    wv = jnp.concatenate([w_in[:, 2 * D_A:fcol], w_in[:, bcol + 2 * D_B:]], axis=1).astype(bf)
    wf = jnp.pad(w_in[:, fcol:bcol], ((0, 0), (0, LANES - N_HEADS_A))).astype(bf)
    b_row = jnp.pad(b_forget, (0, LANES - N_HEADS_A)).reshape(1, LANES)

    qk, vt, kaug = _proj(x, wqk, wv, wf, b_row)
    ya = _fox(qk, kaug, vt)
    yb = _band(qk, vt, _band_bias(rel_bias))
    return _ffn(ya, yb, x, w_out.astype(bf), ln1_g.reshape(1, d), ln1_b.reshape(1, d),
                w_up.astype(bf), conv_w, conv_b.reshape(1, -1), w_down.astype(bf),
                ln2_g.reshape(1, d), ln2_b.reshape(1, d))


def kernel(x, w_in, b_forget, rel_bias, w_out, ln1_g, ln1_b, w_up, conv_w, conv_b, w_down, ln2_g, ln2_b):
    for l in range(DEPTH):
        x = _layer(x, w_in[l], b_forget[l], rel_bias[l], w_out[l], ln1_g[l], ln1_b[l], w_up[l],
                   conv_w[l], conv_b[l], w_down[l], ln2_g[l], ln2_b[l])
    return x
```

```python
import math

import jax
import jax.numpy as jnp
import numpy as np
from jax import lax
from jax.experimental import pallas as pl
from jax.experimental.pallas import tpu as pltpu

D_MODEL = 1024
HEAD_DIM = 64
N_HEADS_A = 8
N_HEADS_B = 8
D_A = N_HEADS_A * HEAD_DIM
D_B = N_HEADS_B * HEAD_DIM
CHUNK = 64
N_LEFT_CHUNKS = 8
REL_CLIP = 128
D_FF = 2816
LN_EPS = 1e-5
DEPTH = 1
DEEPNORM_ALPHA = (2.0 * DEPTH) ** 0.25

LANES = 128
PAIR = 2 * HEAD_DIM
N_PAIRS_A = D_A // PAIR
N_PAIRS_B = D_B // PAIR
ONES_ROWS = 16
NEG = -1e30
LOG2E = math.log2(math.e)
VMEM_LIMIT = 52 * 1024 * 1024

PROJ_TM = 512
FOX_TQ = 2048
FOX_TK = 512
FOX_STRIP = 256
BAND_HALO = N_LEFT_CHUNKS * CHUNK
BAND_TQ = 2048
BAND_G = 256
BAND_KW = BAND_G + BAND_HALO
BAND_SKEW = 2
FFN_TM = 512
FFN_CW = 256


def _params(sem, flags=None):
    return pltpu.CompilerParams(dimension_semantics=sem, vmem_limit_bytes=VMEM_LIMIT, flags=flags)


def _split3(v):
    h1 = v.astype(jnp.bfloat16).astype(jnp.float32)
    r1 = v - h1
    h2 = r1.astype(jnp.bfloat16).astype(jnp.float32)
    h3 = r1 - h2
    return h1, h2, h3


def _proj_kernel(x_ref, wqk_ref, wv_ref, wf_ref, b_ref, tri_ref, place_ref, qk_ref, vt_ref, ka_ref, carry_ref):
    i = pl.program_id(1)

    @pl.when(i == 0)
    def _():
        carry_ref[...] = jnp.zeros_like(carry_ref)

    xb = x_ref[...].astype(jnp.bfloat16)
    tm = xb.shape[0]
    z = jnp.dot(xb, wf_ref[...], preferred_element_type=jnp.float32) + b_ref[...]
    logf = -(jnp.maximum(-z, 0.0) + jnp.log1p(jnp.exp(-jnp.abs(z))))
    parts = jnp.concatenate(_split3(logf), axis=1).astype(jnp.bfloat16)

    qk = jnp.dot(xb, wqk_ref[...], preferred_element_type=jnp.float32)
    scale = HEAD_DIM ** -0.5 * LOG2E
    col = lax.broadcasted_iota(jnp.int32, (1, qk.shape[1]), 1)
    is_q = (col < D_A) | ((col >= 2 * D_A) & (col < 2 * D_A + D_B))
    qk_ref[...] = (qk * jnp.where(is_q, scale, 1.0)).astype(jnp.bfloat16)

    sums = jnp.dot(tri_ref[...], parts, preferred_element_type=jnp.float32)
    cum = carry_ref[0:1, :] + (sums[:, 0:LANES] + sums[:, LANES:2 * LANES] + sums[:, 2 * LANES:])
    carry_ref[...] = jnp.broadcast_to(cum[tm - 1:tm, :], carry_ref.shape)
    neg = cum * -LOG2E
    nparts = jnp.concatenate(_split3(neg), axis=1).astype(jnp.bfloat16)

    v = jnp.dot(xb, wv_ref[...], preferred_element_type=jnp.float32)
    vt_ref[...] = v.T.astype(jnp.bfloat16)

    ka = jnp.dot(nparts, place_ref[...], preferred_element_type=jnp.float32).astype(jnp.bfloat16)
    for hp in range(N_PAIRS_A):
        ka_ref[hp] = ka[:, hp * LANES:(hp + 1) * LANES]


def _gate_constants():
    tri = np.tril(np.ones((PROJ_TM, PROJ_TM), np.float32))
    place = np.zeros((3 * LANES, N_PAIRS_A * LANES), np.float32)
    for hp in range(N_PAIRS_A):
        for j in range(2):
            for t in range(3):
                place[t * LANES + 2 * hp + j, hp * LANES + 3 * j + t] = 1.0
    return jnp.asarray(tri, jnp.bfloat16), jnp.asarray(place, jnp.bfloat16)


def _proj(x, wqk, wv, wf, b_row):
    b, s, d = x.shape
    tm = PROJ_TM
    nqk = wqk.shape[1]
    nv = wv.shape[1]
    tri, place = _gate_constants()
    const = lambda arr: pl.BlockSpec(arr.shape, lambda bi, i: (0, 0))
    return pl.pallas_call(
        _proj_kernel,
        grid=(b, s // tm),
        in_specs=[
            pl.BlockSpec((None, tm, d), lambda bi, i: (bi, i, 0)),
            const(wqk), const(wv), const(wf), const(b_row), const(tri), const(place),
        ],
        out_specs=[
            pl.BlockSpec((None, tm, nqk), lambda bi, i: (bi, i, 0)),
            pl.BlockSpec((None, nv, tm), lambda bi, i: (bi, 0, i)),
            pl.BlockSpec((None, N_PAIRS_A, tm, LANES), lambda bi, i: (bi, 0, i, 0)),
        ],
        out_shape=[
            jax.ShapeDtypeStruct((b, s, nqk), jnp.bfloat16),
            jax.ShapeDtypeStruct((b, nv, s), jnp.bfloat16),
            jax.ShapeDtypeStruct((b, N_PAIRS_A, s, LANES), jnp.bfloat16),
        ],
        scratch_shapes=[pltpu.VMEM((8, LANES), jnp.float32)],
        compiler_params=_params(("parallel", "arbitrary")),
        name="proj",
    )(x, wqk, wv, wf, b_row, tri, place)


def _finalize(acc_e, acc_o):
    oe = acc_e[0:HEAD_DIM, :] / acc_e[HEAD_DIM:HEAD_DIM + 1, :]
    oo = acc_o[0:HEAD_DIM, :] / acc_o[HEAD_DIM:HEAD_DIM + 1, :]
    return jnp.concatenate([oe, oo], axis=0).T


def _vt_with_ones(vt_pair, h):
    ones_rows = jnp.ones((ONES_ROWS, vt_pair.shape[1]), jnp.bfloat16)
    return jnp.concatenate([vt_pair[h * HEAD_DIM:(h + 1) * HEAD_DIM, :], ones_rows], axis=0)


def _fox_kernel(q_ref, k_ref, ka_ref, vt_ref, o_ref, qe_ref, qo_ref, sa_ref, sb_ref, ma_ref, mb_ref,
                acc_e, acc_o, m_e, m_o):
    qi = pl.program_id(2)
    tq, tk = FOX_TQ, FOX_TK
    row = lax.broadcasted_iota(jnp.int32, (LANES, tq), 0)
    qt = q_ref[...].astype(jnp.float32).T
    qe_ref[0:LANES, :] = jnp.where(row < HEAD_DIM, qt, 0.0).astype(jnp.bfloat16)
    qo_ref[0:LANES, :] = jnp.where(row < HEAD_DIM, 0.0, qt).astype(jnp.bfloat16)
    qe_ref[LANES:, :] = jnp.where(row < 3, 1.0, 0.0).astype(jnp.bfloat16)
    qo_ref[LANES:, :] = jnp.where((row >= 3) & (row < 6), 1.0, 0.0).astype(jnp.bfloat16)
    acc_e[...] = jnp.zeros_like(acc_e)
    acc_o[...] = jnp.zeros_like(acc_o)
    m_e[...] = jnp.full_like(m_e, NEG)
    m_o[...] = jnp.full_like(m_o, NEG)

    q_refs = (qe_ref, qo_ref)
    stats = ((acc_e, m_e), (acc_o, m_o))

    def key_operands(kb):
        ks = pl.multiple_of(kb * tk, tk)
        kcat = jnp.concatenate([k_ref[pl.ds(ks, tk), :], ka_ref[pl.ds(ks, tk), :]], axis=1)
        return kcat, vt_ref[:, pl.ds(ks, tk)]

    def score_strip(kcat, h, c, s_ref, mx_ref):
        cs = slice(c, c + FOX_STRIP)
        st = jnp.dot(kcat, q_refs[h][:, cs], preferred_element_type=jnp.float32)
        s_ref[h, :, cs] = st
        mx_ref[h, :, cs] = jnp.max(st, axis=0, keepdims=True)

    tri_r = lax.broadcasted_iota(jnp.int32, (FOX_STRIP, FOX_STRIP), 0)
    tri_c = lax.broadcasted_iota(jnp.int32, (FOX_STRIP, FOX_STRIP), 1)
    tri = jnp.where(tri_r <= tri_c, 0.0, NEG)

    def update_strip(vt_h, h, c, s_ref, mx_ref, mask_c0=None):
        acc, m = stats[h]
        cs = slice(c, c + FOX_STRIP)
        if mask_c0 is None:
            st = s_ref[h, :, cs]
            blk_max = mx_ref[h, :, cs]
        else:
            d = c - mask_c0
            diag = s_ref[h, d:d + FOX_STRIP, cs] + tri
            st = diag if d == 0 else jnp.concatenate([s_ref[h, 0:d, cs], diag], axis=0)
            vt_h = vt_h[:, 0:d + FOX_STRIP]
            blk_max = jnp.max(st, axis=0, keepdims=True)
        m_old = m[:, cs]
        m_new = jnp.maximum(m_old, blk_max)
        p = jnp.exp2(st - m_new).astype(jnp.bfloat16)
        pv = jnp.dot(vt_h, p, preferred_element_type=jnp.float32)
        acc[:, cs] = acc[:, cs] * jnp.exp2(m_old - m_new) + pv
        m[:, cs] = m_new

    def scores(kb, s_ref, mx_ref, c0=0, ncol=tq):
        kcat, _ = key_operands(kb)
        for h in range(2):
            for c in range(c0, c0 + ncol, FOX_STRIP):
                score_strip(kcat, h, c, s_ref, mx_ref)

    def step(kb_next, s_next, mx_next, kb_cur, s_cur, mx_cur):
        kcat, _ = key_operands(kb_next)
        _, vt_pair = key_operands(kb_cur)
        for h in range(2):
            vt_h = _vt_with_ones(vt_pair, h)
            for c in range(0, tq, FOX_STRIP):
                score_strip(kcat, h, c, s_next, mx_next)
                update_strip(vt_h, h, c, s_cur, mx_cur)

    n_own = tq // tk
    kb0 = n_own * qi
    scores(0, sa_ref, ma_ref)

    def pair(kb):
        step(kb + 1, sb_ref, mb_ref, kb, sa_ref, ma_ref)
        step(kb + 2, sa_ref, ma_ref, kb + 1, sb_ref, mb_ref)

    def body(j, carry):
        pair(4 * j)
        pair(4 * j + 2)
        return carry

    n_pairs = kb0 // 2
    lax.fori_loop(0, n_pairs // 2, body, 0)
    if n_own % 4:
        @pl.when(n_pairs % 2 == 1)
        def _():
            pair(kb0 - 2)
    bufs = ((sa_ref, ma_ref), (sb_ref, mb_ref))
    for j in range(n_own):
        s_cur, mx_cur = bufs[j % 2]
        s_nxt, mx_nxt = bufs[(j + 1) % 2]
        _, vt_pair = key_operands(kb0 + j)
        kcat_nxt = key_operands(kb0 + j + 1)[0] if j + 1 < n_own else None
        for h in range(2):
            vt_h = _vt_with_ones(vt_pair, h)
            for c in range((j + 1) * tk, tq, FOX_STRIP):
                score_strip(kcat_nxt, h, c, s_nxt, mx_nxt)
                update_strip(vt_h, h, c, s_cur, mx_cur)
            for c in range(j * tk, (j + 1) * tk, FOX_STRIP):
                update_strip(vt_h, h, c, s_cur, mx_cur, mask_c0=j * tk)
    o_ref[...] = _finalize(acc_e[...], acc_o[...]).astype(o_ref.dtype)


def _fox(qk, kaug, vt):
    b, s, _ = qk.shape
    tq, tk = FOX_TQ, FOX_TK
    kcol0 = D_A // LANES
    return pl.pallas_call(
        _fox_kernel,
        grid=(b, N_PAIRS_A, s // tq),
        in_specs=[
            pl.BlockSpec((None, tq, LANES), lambda bi, hp, qi: (bi, qi, hp)),
            pl.BlockSpec((None, s, LANES), lambda bi, hp, qi: (bi, 0, kcol0 + hp)),
            pl.BlockSpec((None, None, s, LANES), lambda bi, hp, qi: (bi, hp, 0, 0)),
            pl.BlockSpec((None, LANES, s), lambda bi, hp, qi: (bi, hp, 0)),
        ],
        out_specs=pl.BlockSpec((None, tq, LANES), lambda bi, hp, qi: (bi, qi, hp)),
        out_shape=jax.ShapeDtypeStruct((b, s, D_A), jnp.bfloat16),
        scratch_shapes=[
            pltpu.VMEM((2 * LANES, tq), jnp.bfloat16),
            pltpu.VMEM((2 * LANES, tq), jnp.bfloat16),
            pltpu.VMEM((2, tk, tq), jnp.float32),
            pltpu.VMEM((2, tk, tq), jnp.float32),
            pltpu.VMEM((2, 1, tq), jnp.float32),
            pltpu.VMEM((2, 1, tq), jnp.float32),
            pltpu.VMEM((HEAD_DIM + ONES_ROWS, tq), jnp.float32),
            pltpu.VMEM((HEAD_DIM + ONES_ROWS, tq), jnp.float32),
            pltpu.VMEM((1, tq), jnp.float32),
            pltpu.VMEM((1, tq), jnp.float32),
        ],
        compiler_params=_params(("parallel", "parallel", "arbitrary")),
        name="fox",
    )(qk, qk, kaug, vt)


def _band_kernel(q_ref, kp_ref, kc_ref, vtp_ref, vtc_ref, bias_ref, o_ref, kwin, vtwin, s_scr, p_scr):
    i = pl.program_id(2)
    tq, halo = BAND_TQ, BAND_HALO
    n_units = 2 * (tq // BAND_G)
    lane = lax.broadcasted_iota(jnp.int32, (halo, LANES), 1)
    kwin[0:halo, 0:LANES] = kp_ref[...]
    kwin[halo:, 0:LANES] = kc_ref[...]
    pen = jnp.where(i == 0, NEG, 0.0)
    kwin[0:halo, LANES:] = jnp.where(lane == 0, pen, 0.0).astype(jnp.bfloat16)
    kwin[halo:, LANES:] = jnp.zeros((tq, LANES), jnp.bfloat16)
    vtwin[:, 0:halo] = vtp_ref[...]
    vtwin[:, halo:] = vtc_ref[...]
    row_g = lax.broadcasted_iota(jnp.int32, (LANES, BAND_G), 0)
    ones_row = jnp.where(row_g == 0, 1.0, 0.0).astype(jnp.bfloat16)

    def scores(u):
        g, h = divmod(u, 2)
        qt = q_ref[g * BAND_G:(g + 1) * BAND_G, :].astype(jnp.float32).T
        own = (row_g < HEAD_DIM) if h == 0 else (row_g >= HEAD_DIM)
        qcat = jnp.concatenate([jnp.where(own, qt, 0.0).astype(jnp.bfloat16), ones_row], axis=0)
        kslab = kwin[g * BAND_G:g * BAND_G + BAND_KW, :]
        s_scr[u] = jnp.dot(kslab, qcat, preferred_element_type=jnp.float32)

    def softmax(u):
        st = s_scr[u] + bias_ref[u % 2]
        m = jnp.max(st, axis=0, keepdims=True)
        p_scr[u] = jnp.exp2(st - m).astype(jnp.bfloat16)

    accs = {}

    def pv(u):
        g, h = divmod(u, 2)
        vslab = vtwin[:, g * BAND_G:g * BAND_G + BAND_KW]
        accs[h] = jnp.dot(_vt_with_ones(vslab, h), p_scr[u], preferred_element_type=jnp.float32)
        if h == 1:
            o_ref[g * BAND_G:(g + 1) * BAND_G, :] = _finalize(accs[0], accs[1]).astype(o_ref.dtype)

    for t in range(n_units + 2 * BAND_SKEW):
        if t < n_units:
            scores(t)
        if BAND_SKEW <= t < n_units + BAND_SKEW:
            softmax(t - BAND_SKEW)
        if t >= 2 * BAND_SKEW:
            pv(t - 2 * BAND_SKEW)


def _band(qk, vt, bias_t):
    b, s, _ = qk.shape
    tq, halo = BAND_TQ, BAND_HALO
    r = tq // halo
    qcol0 = 2 * D_A // LANES
    kcol0 = (2 * D_A + D_B) // LANES
    vrow0 = D_A // PAIR
    prev = lambda i: jnp.maximum(i * r - 1, 0)
    n_units = 2 * (tq // BAND_G)
    return pl.pallas_call(
        _band_kernel,
        grid=(b, N_PAIRS_B, s // tq),
        in_specs=[
            pl.BlockSpec((None, tq, LANES), lambda bi, hp, i: (bi, i, qcol0 + hp)),
            pl.BlockSpec((None, halo, LANES), lambda bi, hp, i: (bi, prev(i), kcol0 + hp)),
            pl.BlockSpec((None, tq, LANES), lambda bi, hp, i: (bi, i, kcol0 + hp)),
            pl.BlockSpec((None, PAIR, halo), lambda bi, hp, i: (bi, vrow0 + hp, prev(i))),
            pl.BlockSpec((None, PAIR, tq), lambda bi, hp, i: (bi, vrow0 + hp, i)),
            pl.BlockSpec((2, BAND_KW, BAND_G), lambda bi, hp, i: (hp, 0, 0)),
        ],
        out_specs=pl.BlockSpec((None, tq, LANES), lambda bi, hp, i: (bi, i, hp)),
        out_shape=jax.ShapeDtypeStruct((b, s, D_B), jnp.bfloat16),
        scratch_shapes=[
            pltpu.VMEM((halo + tq, 2 * LANES), jnp.bfloat16),
            pltpu.VMEM((PAIR, halo + tq), jnp.bfloat16),
            pltpu.VMEM((n_units, BAND_KW, BAND_G), jnp.float32),
            pltpu.VMEM((n_units, BAND_KW, BAND_G), jnp.bfloat16),
        ],
        compiler_params=_params(("parallel", "parallel", "arbitrary")),
        name="band",
    )(qk, qk, qk, vt, vt, bias_t)


BIAS_W = 1024


def _bias_kernel(g_ref, o_ref):
    row = jnp.broadcast_to(g_ref[...], (BAND_KW, BIAS_W))
    toe = pltpu.roll(row, 0, 1, stride=1, stride_axis=0)[:, 0:BAND_G]
    kc = lax.broadcasted_iota(jnp.int32, (BAND_KW, BAND_G), 0) // CHUNK
    qc = lax.broadcasted_iota(jnp.int32, (BAND_KW, BAND_G), 1) // CHUNK
    inband = (kc >= qc) & (kc <= qc + N_LEFT_CHUNKS)
    o_ref[...] = jnp.where(inband, toe * LOG2E, NEG)


def _band_bias(rel_bias):
    h = rel_bias.shape[0]
    j = np.arange(BIAS_W)
    off = np.where(j < BAND_G, j, j - BIAS_W)
    idx = np.clip(N_LEFT_CHUNKS * CHUNK + off, -REL_CLIP, REL_CLIP) + REL_CLIP
    g_ext = rel_bias.astype(jnp.float32)[:, idx].reshape(h, 1, BIAS_W)
    return pl.pallas_call(
        _bias_kernel,
        grid=(h,),
        in_specs=[pl.BlockSpec((None, 1, BIAS_W), lambda i: (i, 0, 0))],
        out_specs=pl.BlockSpec((None, BAND_KW, BAND_G), lambda i: (i, 0, 0)),
        out_shape=jax.ShapeDtypeStruct((h, BAND_KW, BAND_G), jnp.float32),
        compiler_params=_params(("parallel",)),
        name="bias",
    )(g_ext)


def _layer_norm(z, g, b):
    mu = jnp.mean(z, axis=-1, keepdims=True)
    zc = z - mu
    var = jnp.mean(zc * zc, axis=-1, keepdims=True)
    return zc * lax.rsqrt(var + LN_EPS) * g + b


def _gelu_tanh(x):
    return 0.5 * x * (1.0 + jnp.tanh(math.sqrt(2.0 / math.pi) * (x + 0.044715 * (x * x * x))))


def _causal_conv(u, carry, w, bias):
    tm = u.shape[0]
    r8 = lax.broadcasted_iota(jnp.int32, (8, u.shape[1]), 0)
    outs = w[2:3, :] * u + bias
    for shift, j in ((1, 1), (2, 0)):
        ur = pltpu.roll(u, shift, 0)
        cr = pltpu.roll(carry, shift, 0)
        head = jnp.where(r8 < shift, cr, ur[0:8, :])
        us = jnp.concatenate([head, ur[8:tm, :]], axis=0)
        outs = outs + w[j:j + 1, :] * us
    return outs


def _ffn_kernel(ya_ref, yb_ref, x_ref, wout_ref, g1_ref, b1_ref, wup_ref, cw_ref, cb_ref, wdn_ref, g2_ref, b2_ref,
                o_ref, carry_ref, h_ref):
    i = pl.program_id(1)

    @pl.when(i == 0)
    def _():
        carry_ref[...] = jnp.zeros_like(carry_ref)

    y = jnp.concatenate([ya_ref[...], yb_ref[...]], axis=1)
    mix = jnp.dot(y, wout_ref[...], preferred_element_type=jnp.float32)
    x1 = _layer_norm(DEEPNORM_ALPHA * x_ref[...] + mix, g1_ref[...], b1_ref[...])
    xb = x1.astype(jnp.bfloat16)
    tm = x1.shape[0]
    for c in range(D_FF // FFN_CW):
        halves = []
        for off in (c * FFN_CW, D_FF + c * FFN_CW):
            sl = slice(off, off + FFN_CW)
            u = jnp.dot(xb, wup_ref[:, sl], preferred_element_type=jnp.float32)
            halves.append(_causal_conv(u, carry_ref[:, sl], cw_ref[:, sl], cb_ref[:, sl]))
            carry_ref[:, sl] = u[tm - 8:tm, :]
        h_ref[:, c * FFN_CW:(c + 1) * FFN_CW] = (halves[0] * _gelu_tanh(halves[1])).astype(jnp.bfloat16)
    ffn = jnp.dot(h_ref[...], wdn_ref[...], preferred_element_type=jnp.float32)
    o_ref[...] = _layer_norm(DEEPNORM_ALPHA * x1 + ffn, g2_ref[...], b2_ref[...])


def _ffn(ya, yb, x, w_out, g1, b1, w_up, conv_w, conv_b, w_down, g2, b2):
    bsz, s, d = x.shape
    tm = FFN_TM
    nu = w_up.shape[1]
    const = lambda shape: pl.BlockSpec(shape, lambda bi, i: (0, 0), pipeline_mode=pl.Buffered(1))
    rows = lambda width: pl.BlockSpec((None, tm, width), lambda bi, i: (bi, i, 0))
    return pl.pallas_call(
        _ffn_kernel,
        grid=(bsz, s // tm),
        in_specs=[
            rows(ya.shape[2]), rows(yb.shape[2]), rows(d),
            const((d, d)), const((1, d)), const((1, d)),
            const((d, nu)), const((3, nu)), const((1, nu)), const((D_FF, d)), const((1, d)), const((1, d)),
        ],
        out_specs=rows(d),
        out_shape=jax.ShapeDtypeStruct((bsz, s, d), jnp.float32),
        scratch_shapes=[
            pltpu.VMEM((8, nu), jnp.float32),
            pltpu.VMEM((tm, D_FF), jnp.bfloat16),
        ],
        compiler_params=_params(("parallel", "arbitrary")),
        name="ffn",
    )(ya, yb, x, w_out, g1, b1, w_up, conv_w, conv_b, w_down, g2, b2)


def _layer(x, w_in, b_forget, rel_bias, w_out, ln1_g, ln1_b, w_up, conv_w, conv_b, w_down, ln2_g, ln2_b):
    b, s, d = x.shape
    bf = jnp.bfloat16
    fcol = 3 * D_A
    bcol = fcol + N_HEADS_A
    wqk = jnp.concatenate([w_in[:, 0:2 * D_A], w_in[:, bcol:bcol + 2 * D_B]], axis=1).astype(bf)
    wv = jnp.concatenate([w_in[:, 2 * D_A:fcol], w_in[:, bcol + 2 * D_B:]], axis=1).astype(bf)
    wf = jnp.pad(w_in[:, fcol:bcol], ((0, 0), (0, LANES - N_HEADS_A))).astype(bf)
    b_row = jnp.pad(b_forget, (0, LANES - N_HEADS_A)).reshape(1, LANES)

    qk, vt, kaug = _proj(x, wqk, wv, wf, b_row)
    ya = _fox(qk, kaug, vt)
    yb = _band(qk, vt, _band_bias(rel_bias))
    return _ffn(ya, yb, x, w_out.astype(bf), ln1_g.reshape(1, d), ln1_b.reshape(1, d),
                w_up.astype(bf), conv_w, conv_b.reshape(1, -1), w_down.astype(bf),
                ln2_g.reshape(1, d), ln2_b.reshape(1, d))


def kernel(x, w_in, b_forget, rel_bias, w_out, ln1_g, ln1_b, w_up, conv_w, conv_b, w_down, ln2_g, ln2_b):
    for l in range(DEPTH):
        x = _layer(x, w_in[l], b_forget[l], rel_bias[l], w_out[l], ln1_g[l], ln1_b[l], w_up[l],
                   conv_w[l], conv_b[l], w_down[l], ln2_g[l], ln2_b[l])
    return x
```

```python
import math

import jax
import jax.numpy as jnp
import numpy as np
from jax import lax
from jax.experimental import pallas as pl
from jax.experimental.pallas import tpu as pltpu

D_MODEL = 1024
HEAD_DIM = 64
N_HEADS_A = 8
N_HEADS_B = 8
D_A = N_HEADS_A * HEAD_DIM
D_B = N_HEADS_B * HEAD_DIM
CHUNK = 64
N_LEFT_CHUNKS = 8
REL_CLIP = 128
D_FF = 2816
LN_EPS = 1e-5
DEPTH = 1
DEEPNORM_ALPHA = (2.0 * DEPTH) ** 0.25

LANES = 128
PAIR = 2 * HEAD_DIM
N_PAIRS_A = D_A // PAIR
N_PAIRS_B = D_B // PAIR
ONES_ROWS = 16
NEG = -1e30
LOG2E = math.log2(math.e)
VMEM_LIMIT = 52 * 1024 * 1024

PROJ_TM = 512
FOX_TQ = 2048
FOX_TK = 512
FOX_STRIP = 256
BAND_HALO = N_LEFT_CHUNKS * CHUNK
BAND_TQ = 2048
BAND_G = 256
BAND_KW = BAND_G + BAND_HALO
BAND_SKEW = 2
FFN_TM = 512
FFN_CW = 256


def _params(sem):
    return pltpu.CompilerParams(dimension_semantics=sem, vmem_limit_bytes=VMEM_LIMIT)


def _split3(v):
    h1 = v.astype(jnp.bfloat16).astype(jnp.float32)
    r1 = v - h1
    h2 = r1.astype(jnp.bfloat16).astype(jnp.float32)
    h3 = r1 - h2
    return h1, h2, h3


def _proj_kernel(x_ref, wqk_ref, wv_ref, wf_ref, b_ref, tri_ref, place_ref, qk_ref, vt_ref, ka_ref, carry_ref):
    i = pl.program_id(1)

    @pl.when(i == 0)
    def _():
        carry_ref[...] = jnp.zeros_like(carry_ref)

    xb = x_ref[...].astype(jnp.bfloat16)
    tm = xb.shape[0]
    z = jnp.dot(xb, wf_ref[...], preferred_element_type=jnp.float32) + b_ref[...]
    logf = -(jnp.maximum(-z, 0.0) + jnp.log1p(jnp.exp(-jnp.abs(z))))
    parts = jnp.concatenate(_split3(logf), axis=1).astype(jnp.bfloat16)

    qk = jnp.dot(xb, wqk_ref[...], preferred_element_type=jnp.float32)
    scale = HEAD_DIM ** -0.5 * LOG2E
    col = lax.broadcasted_iota(jnp.int32, (1, qk.shape[1]), 1)
    is_q = (col < D_A) | ((col >= 2 * D_A) & (col < 2 * D_A + D_B))
    qk_ref[...] = (qk * jnp.where(is_q, scale, 1.0)).astype(jnp.bfloat16)

    sums = jnp.dot(tri_ref[...], parts, preferred_element_type=jnp.float32)
    cum = carry_ref[0:1, :] + (sums[:, 0:LANES] + sums[:, LANES:2 * LANES] + sums[:, 2 * LANES:])
    carry_ref[...] = jnp.broadcast_to(cum[tm - 1:tm, :], carry_ref.shape)
    neg = cum * -LOG2E
    nparts = jnp.concatenate(_split3(neg), axis=1).astype(jnp.bfloat16)

    v = jnp.dot(xb, wv_ref[...], preferred_element_type=jnp.float32)
    vt_ref[...] = v.T.astype(jnp.bfloat16)

    ka = jnp.dot(nparts, place_ref[...], preferred_element_type=jnp.float32).astype(jnp.bfloat16)
    for hp in range(N_PAIRS_A):
        ka_ref[hp] = ka[:, hp * LANES:(hp + 1) * LANES]


def _gate_constants():
    tri = np.tril(np.ones((PROJ_TM, PROJ_TM), np.float32))
    place = np.zeros((3 * LANES, N_PAIRS_A * LANES), np.float32)
    for hp in range(N_PAIRS_A):
        for j in range(2):
            for t in range(3):
                place[t * LANES + 2 * hp + j, hp * LANES + 3 * j + t] = 1.0
    return jnp.asarray(tri, jnp.bfloat16), jnp.asarray(place, jnp.bfloat16)


def _proj(x, wqk, wv, wf, b_row):
    b, s, d = x.shape
    tm = PROJ_TM
    nqk = wqk.shape[1]
    nv = wv.shape[1]
    tri, place = _gate_constants()
    const = lambda arr: pl.BlockSpec(arr.shape, lambda bi, i: (0, 0))
    return pl.pallas_call(
        _proj_kernel,
        grid=(b, s // tm),
        in_specs=[
            pl.BlockSpec((None, tm, d), lambda bi, i: (bi, i, 0)),
            const(wqk), const(wv), const(wf), const(b_row), const(tri), const(place),
        ],
        out_specs=[
            pl.BlockSpec((None, tm, nqk), lambda bi, i: (bi, i, 0)),
            pl.BlockSpec((None, nv, tm), lambda bi, i: (bi, 0, i)),
            pl.BlockSpec((None, N_PAIRS_A, tm, LANES), lambda bi, i: (bi, 0, i, 0)),
        ],
        out_shape=[
            jax.ShapeDtypeStruct((b, s, nqk), jnp.bfloat16),
            jax.ShapeDtypeStruct((b, nv, s), jnp.bfloat16),
            jax.ShapeDtypeStruct((b, N_PAIRS_A, s, LANES), jnp.bfloat16),
        ],
        scratch_shapes=[pltpu.VMEM((8, LANES), jnp.float32)],
        compiler_params=_params(("parallel", "arbitrary")),
        name="proj",
    )(x, wqk, wv, wf, b_row, tri, place)


def _finalize(acc_e, acc_o):
    oe = acc_e[0:HEAD_DIM, :] / acc_e[HEAD_DIM:HEAD_DIM + 1, :]
    oo = acc_o[0:HEAD_DIM, :] / acc_o[HEAD_DIM:HEAD_DIM + 1, :]
    return jnp.concatenate([oe, oo], axis=0).T


def _vt_with_ones(vt_pair, h):
    ones_rows = jnp.ones((ONES_ROWS, vt_pair.shape[1]), jnp.bfloat16)
    return jnp.concatenate([vt_pair[h * HEAD_DIM:(h + 1) * HEAD_DIM, :], ones_rows], axis=0)


def _fox_kernel(q_ref, k_ref, ka_ref, vt_ref, o_ref, qe_ref, qo_ref, sa_ref, sb_ref, ma_ref, mb_ref,
                acc_e, acc_o, m_e, m_o):
    qi = pl.program_id(2)
    tq, tk = FOX_TQ, FOX_TK
    row = lax.broadcasted_iota(jnp.int32, (LANES, tq), 0)
    qt = q_ref[...].astype(jnp.float32).T
    qe_ref[0:LANES, :] = jnp.where(row < HEAD_DIM, qt, 0.0).astype(jnp.bfloat16)
    qo_ref[0:LANES, :] = jnp.where(row < HEAD_DIM, 0.0, qt).astype(jnp.bfloat16)
    qe_ref[LANES:, :] = jnp.where(row < 3, 1.0, 0.0).astype(jnp.bfloat16)
    qo_ref[LANES:, :] = jnp.where((row >= 3) & (row < 6), 1.0, 0.0).astype(jnp.bfloat16)
    acc_e[...] = jnp.zeros_like(acc_e)
    acc_o[...] = jnp.zeros_like(acc_o)
    m_e[...] = jnp.full_like(m_e, NEG)
    m_o[...] = jnp.full_like(m_o, NEG)

    q_refs = (qe_ref, qo_ref)
    stats = ((acc_e, m_e), (acc_o, m_o))

    def key_operands(kb):
        ks = pl.multiple_of(kb * tk, tk)
        kcat = jnp.concatenate([k_ref[pl.ds(ks, tk), :], ka_ref[pl.ds(ks, tk), :]], axis=1)
        return kcat, vt_ref[:, pl.ds(ks, tk)]

    def score_strip(kcat, h, c, s_ref, mx_ref):
        cs = slice(c, c + FOX_STRIP)
        st = jnp.dot(kcat, q_refs[h][:, cs], preferred_element_type=jnp.float32)
        s_ref[h, :, cs] = st
        mx_ref[h, :, cs] = jnp.max(st, axis=0, keepdims=True)

    tri_r = lax.broadcasted_iota(jnp.int32, (FOX_STRIP, FOX_STRIP), 0)
    tri_c = lax.broadcasted_iota(jnp.int32, (FOX_STRIP, FOX_STRIP), 1)
    tri = jnp.where(tri_r <= tri_c, 0.0, NEG)

    def update_strip(vt_h, h, c, s_ref, mx_ref, mask_c0=None):
        acc, m = stats[h]
        cs = slice(c, c + FOX_STRIP)
        if mask_c0 is None:
            st = s_ref[h, :, cs]
            blk_max = mx_ref[h, :, cs]
        else:
            d = c - mask_c0
            diag = s_ref[h, d:d + FOX_STRIP, cs] + tri
            st = diag if d == 0 else jnp.concatenate([s_ref[h, 0:d, cs], diag], axis=0)
            vt_h = vt_h[:, 0:d + FOX_STRIP]
            blk_max = jnp.max(st, axis=0, keepdims=True)
        m_old = m[:, cs]
        m_new = jnp.maximum(m_old, blk_max)
        p = jnp.exp2(st - m_new).astype(jnp.bfloat16)
        pv = jnp.dot(vt_h, p, preferred_element_type=jnp.float32)
        acc[:, cs] = acc[:, cs] * jnp.exp2(m_old - m_new) + pv
        m[:, cs] = m_new

    def scores(kb, s_ref, mx_ref, c0=0, ncol=tq):
        kcat, _ = key_operands(kb)
        for h in range(2):
            for c in range(c0, c0 + ncol, FOX_STRIP):
                score_strip(kcat, h, c, s_ref, mx_ref)

    def step(kb_next, s_next, mx_next, kb_cur, s_cur, mx_cur):
        kcat, _ = key_operands(kb_next)
        _, vt_pair = key_operands(kb_cur)
        for h in range(2):
            vt_h = _vt_with_ones(vt_pair, h)
            for c in range(0, tq, FOX_STRIP):
                score_strip(kcat, h, c, s_next, mx_next)
                update_strip(vt_h, h, c, s_cur, mx_cur)

    n_own = tq // tk
    kb0 = n_own * qi
    scores(0, sa_ref, ma_ref)

    def pair(kb):
        step(kb + 1, sb_ref, mb_ref, kb, sa_ref, ma_ref)
        step(kb + 2, sa_ref, ma_ref, kb + 1, sb_ref, mb_ref)

    def body(j, carry):
        pair(4 * j)
        pair(4 * j + 2)
        return carry

    n_pairs = kb0 // 2
    lax.fori_loop(0, n_pairs // 2, body, 0)
    if n_own % 4:
        @pl.when(n_pairs % 2 == 1)
        def _():
            pair(kb0 - 2)
    bufs = ((sa_ref, ma_ref), (sb_ref, mb_ref))
    for j in range(n_own):
        s_cur, mx_cur = bufs[j % 2]
        s_nxt, mx_nxt = bufs[(j + 1) % 2]
        _, vt_pair = key_operands(kb0 + j)
        kcat_nxt = key_operands(kb0 + j + 1)[0] if j + 1 < n_own else None
        for h in range(2):
            vt_h = _vt_with_ones(vt_pair, h)
            for c in range((j + 1) * tk, tq, FOX_STRIP):
                score_strip(kcat_nxt, h, c, s_nxt, mx_nxt)
                update_strip(vt_h, h, c, s_cur, mx_cur)
            for c in range(j * tk, (j + 1) * tk, FOX_STRIP):
                update_strip(vt_h, h, c, s_cur, mx_cur, mask_c0=j * tk)
    o_ref[...] = _finalize(acc_e[...], acc_o[...]).astype(o_ref.dtype)


def _fox(qk, kaug, vt):
    b, s, _ = qk.shape
    tq, tk = FOX_TQ, FOX_TK
    kcol0 = D_A // LANES
    return pl.pallas_call(
        _fox_kernel,
        grid=(b, N_PAIRS_A, s // tq),
        in_specs=[
            pl.BlockSpec((None, tq, LANES), lambda bi, hp, qi: (bi, qi, hp)),
            pl.BlockSpec((None, s, LANES), lambda bi, hp, qi: (bi, 0, kcol0 + hp)),
            pl.BlockSpec((None, None, s, LANES), lambda bi, hp, qi: (bi, hp, 0, 0)),
            pl.BlockSpec((None, LANES, s), lambda bi, hp, qi: (bi, hp, 0)),
        ],
        out_specs=pl.BlockSpec((None, tq, LANES), lambda bi, hp, qi: (bi, qi, hp)),
        out_shape=jax.ShapeDtypeStruct((b, s, D_A), jnp.bfloat16),
        scratch_shapes=[
            pltpu.VMEM((2 * LANES, tq), jnp.bfloat16),
            pltpu.VMEM((2 * LANES, tq), jnp.bfloat16),
            pltpu.VMEM((2, tk, tq), jnp.float32),
            pltpu.VMEM((2, tk, tq), jnp.float32),
            pltpu.VMEM((2, 1, tq), jnp.float32),
            pltpu.VMEM((2, 1, tq), jnp.float32),
            pltpu.VMEM((HEAD_DIM + ONES_ROWS, tq), jnp.float32),
            pltpu.VMEM((HEAD_DIM + ONES_ROWS, tq), jnp.float32),
            pltpu.VMEM((1, tq), jnp.float32),
            pltpu.VMEM((1, tq), jnp.float32),
        ],
        compiler_params=_params(("parallel", "parallel", "arbitrary")),
        name="fox",
    )(qk, qk, kaug, vt)


def _band_kernel(q_ref, kp_ref, kc_ref, vtp_ref, vtc_ref, bias_ref, o_ref, kwin, vtwin, s_scr, p_scr):
    i = pl.program_id(2)
    tq, halo = BAND_TQ, BAND_HALO
    n_units = 2 * (tq // BAND_G)
    lane = lax.broadcasted_iota(jnp.int32, (halo, LANES), 1)
    kwin[0:halo, 0:LANES] = kp_ref[...]
    kwin[halo:, 0:LANES] = kc_ref[...]
    pen = jnp.where(i == 0, NEG, 0.0)
    kwin[0:halo, LANES:] = jnp.where(lane == 0, pen, 0.0).astype(jnp.bfloat16)
    kwin[halo:, LANES:] = jnp.zeros((tq, LANES), jnp.bfloat16)
    vtwin[:, 0:halo] = vtp_ref[...]
    vtwin[:, halo:] = vtc_ref[...]
    row_g = lax.broadcasted_iota(jnp.int32, (LANES, BAND_G), 0)
    ones_row = jnp.where(row_g == 0, 1.0, 0.0).astype(jnp.bfloat16)

    def scores(u):
        g, h = divmod(u, 2)
        qt = q_ref[g * BAND_G:(g + 1) * BAND_G, :].astype(jnp.float32).T
        own = (row_g < HEAD_DIM) if h == 0 else (row_g >= HEAD_DIM)
        qcat = jnp.concatenate([jnp.where(own, qt, 0.0).astype(jnp.bfloat16), ones_row], axis=0)
        kslab = kwin[g * BAND_G:g * BAND_G + BAND_KW, :]
        s_scr[u] = jnp.dot(kslab, qcat, preferred_element_type=jnp.float32)

    def softmax(u):
        st = s_scr[u] + bias_ref[u % 2]
        m = jnp.max(st, axis=0, keepdims=True)
        p_scr[u] = jnp.exp2(st - m).astype(jnp.bfloat16)

    accs = {}

    def pv(u):
        g, h = divmod(u, 2)
        vslab = vtwin[:, g * BAND_G:g * BAND_G + BAND_KW]
        accs[h] = jnp.dot(_vt_with_ones(vslab, h), p_scr[u], preferred_element_type=jnp.float32)
        if h == 1:
            o_ref[g * BAND_G:(g + 1) * BAND_G, :] = _finalize(accs[0], accs[1]).astype(o_ref.dtype)

    for t in range(n_units + 2 * BAND_SKEW):
        if t < n_units:
            scores(t)
        if BAND_SKEW <= t < n_units + BAND_SKEW:
            softmax(t - BAND_SKEW)
        if t >= 2 * BAND_SKEW:
            pv(t - 2 * BAND_SKEW)


def _band(qk, vt, bias_t):
    b, s, _ = qk.shape
    tq, halo = BAND_TQ, BAND_HALO
    r = tq // halo
    qcol0 = 2 * D_A // LANES
    kcol0 = (2 * D_A + D_B) // LANES
    vrow0 = D_A // PAIR
    prev = lambda i: jnp.maximum(i * r - 1, 0)
    n_units = 2 * (tq // BAND_G)
    return pl.pallas_call(
        _band_kernel,
        grid=(b, N_PAIRS_B, s // tq),
        in_specs=[
            pl.BlockSpec((None, tq, LANES), lambda bi, hp, i: (bi, i, qcol0 + hp)),
            pl.BlockSpec((None, halo, LANES), lambda bi, hp, i: (bi, prev(i), kcol0 + hp)),
            pl.BlockSpec((None, tq, LANES), lambda bi, hp, i: (bi, i, kcol0 + hp)),
            pl.BlockSpec((None, PAIR, halo), lambda bi, hp, i: (bi, vrow0 + hp, prev(i))),
            pl.BlockSpec((None, PAIR, tq), lambda bi, hp, i: (bi, vrow0 + hp, i)),
            pl.BlockSpec((2, BAND_KW, BAND_G), lambda bi, hp, i: (hp, 0, 0)),
        ],
        out_specs=pl.BlockSpec((None, tq, LANES), lambda bi, hp, i: (bi, i, hp)),
        out_shape=jax.ShapeDtypeStruct((b, s, D_B), jnp.bfloat16),
        scratch_shapes=[
            pltpu.VMEM((halo + tq, 2 * LANES), jnp.bfloat16),
            pltpu.VMEM((PAIR, halo + tq), jnp.bfloat16),
            pltpu.VMEM((n_units, BAND_KW, BAND_G), jnp.float32),
            pltpu.VMEM((n_units, BAND_KW, BAND_G), jnp.bfloat16),
        ],
        compiler_params=_params(("parallel", "parallel", "arbitrary")),
        name="band",
    )(qk, qk, qk, vt, vt, bias_t)


BIAS_W = 1024


def _bias_kernel(g_ref, o_ref):
    row = jnp.broadcast_to(g_ref[...], (BAND_KW, BIAS_W))
    toe = pltpu.roll(row, 0, 1, stride=1, stride_axis=0)[:, 0:BAND_G]
    kc = lax.broadcasted_iota(jnp.int32, (BAND_KW, BAND_G), 0) // CHUNK
    qc = lax.broadcasted_iota(jnp.int32, (BAND_KW, BAND_G), 1) // CHUNK
    inband = (kc >= qc) & (kc <= qc + N_LEFT_CHUNKS)
    o_ref[...] = jnp.where(inband, toe * LOG2E, NEG)


def _band_bias(rel_bias):
    h = rel_bias.shape[0]
    j = np.arange(BIAS_W)
    off = np.where(j < BAND_G, j, j - BIAS_W)
    idx = np.clip(N_LEFT_CHUNKS * CHUNK + off, -REL_CLIP, REL_CLIP) + REL_CLIP
    g_ext = rel_bias.astype(jnp.float32)[:, idx].reshape(h, 1, BIAS_W)
    return pl.pallas_call(
        _bias_kernel,
        grid=(h,),
        in_specs=[pl.BlockSpec((None, 1, BIAS_W), lambda i: (i, 0, 0))],
        out_specs=pl.BlockSpec((None, BAND_KW, BAND_G), lambda i: (i, 0, 0)),
        out_shape=jax.ShapeDtypeStruct((h, BAND_KW, BAND_G), jnp.float32),
        compiler_params=_params(("parallel",)),
        name="bias",
    )(g_ext)


def _layer_norm(z, g, b):
    mu = jnp.mean(z, axis=-1, keepdims=True)
    zc = z - mu
    var = jnp.mean(zc * zc, axis=-1, keepdims=True)
    return zc * lax.rsqrt(var + LN_EPS) * g + b


def _gelu_tanh(x):
    return 0.5 * x * (1.0 + jnp.tanh(math.sqrt(2.0 / math.pi) * (x + 0.044715 * (x * x * x))))


def _causal_conv(u, carry, w, bias):
    tm = u.shape[0]
    r8 = lax.broadcasted_iota(jnp.int32, (8, u.shape[1]), 0)
    outs = w[2:3, :] * u + bias
    for shift, j in ((1, 1), (2, 0)):
        ur = pltpu.roll(u, shift, 0)
        cr = pltpu.roll(carry, shift, 0)
        head = jnp.where(r8 < shift, cr, ur[0:8, :])
        us = jnp.concatenate([head, ur[8:tm, :]], axis=0)
        outs = outs + w[j:j + 1, :] * us
    return outs


def _ffn_kernel(ya_ref, yb_ref, x_ref, wout_ref, g1_ref, b1_ref, wup_ref, cw_ref, cb_ref, wdn_ref, g2_ref, b2_ref,
                o_ref, carry_ref, h_ref):
    i = pl.program_id(1)

    @pl.when(i == 0)
    def _():
        carry_ref[...] = jnp.zeros_like(carry_ref)

    y = jnp.concatenate([ya_ref[...], yb_ref[...]], axis=1)
    mix = jnp.dot(y, wout_ref[...], preferred_element_type=jnp.float32)
    x1 = _layer_norm(DEEPNORM_ALPHA * x_ref[...] + mix, g1_ref[...], b1_ref[...])
    xb = x1.astype(jnp.bfloat16)
    tm = x1.shape[0]
    for c in range(D_FF // FFN_CW):
        halves = []
        for off in (c * FFN_CW, D_FF + c * FFN_CW):
            sl = slice(off, off + FFN_CW)
            u = jnp.dot(xb, wup_ref[:, sl], preferred_element_type=jnp.float32)
            halves.append(_causal_conv(u, carry_ref[:, sl], cw_ref[:, sl], cb_ref[:, sl]))
            carry_ref[:, sl] = u[tm - 8:tm, :]
        h_ref[:, c * FFN_CW:(c + 1) * FFN_CW] = (halves[0] * _gelu_tanh(halves[1])).astype(jnp.bfloat16)
    ffn = jnp.dot(h_ref[...], wdn_ref[...], preferred_element_type=jnp.float32)
    o_ref[...] = _layer_norm(DEEPNORM_ALPHA * x1 + ffn, g2_ref[...], b2_ref[...])


def _ffn(ya, yb, x, w_out, g1, b1, w_up, conv_w, conv_b, w_down, g2, b2):
    bsz, s, d = x.shape
    tm = FFN_TM
    nu = w_up.shape[1]
    const = lambda shape: pl.BlockSpec(shape, lambda bi, i: (0, 0), pipeline_mode=pl.Buffered(1))
    rows = lambda width: pl.BlockSpec((None, tm, width), lambda bi, i: (bi, i, 0))
    return pl.pallas_call(
        _ffn_kernel,
        grid=(bsz, s // tm),
        in_specs=[
            rows(ya.shape[2]), rows(yb.shape[2]), rows(d),
            const((d, d)), const((1, d)), const((1, d)),
            const((d, nu)), const((3, nu)), const((1, nu)), const((D_FF, d)), const((1, d)), const((1, d)),
        ],
        out_specs=rows(d),
        out_shape=jax.ShapeDtypeStruct((bsz, s, d), jnp.float32),
        scratch_shapes=[
            pltpu.VMEM((8, nu), jnp.float32),
            pltpu.VMEM((tm, D_FF), jnp.bfloat16),
        ],
        compiler_params=_params(("parallel", "arbitrary")),
        name="ffn",
    )(ya, yb, x, w_out, g1, b1, w_up, conv_w, conv_b, w_down, g2, b2)


def _layer(x, w_in, b_forget, rel_bias, w_out, ln1_g, ln1_b, w_up, conv_w, conv_b, w_down, ln2_g, ln2_b):
    b, s, d = x.shape
    bf = jnp.bfloat16
    fcol = 3 * D_A
    bcol = fcol + N_HEADS_A
    wqk = jnp.concatenate([w_in[:, 0:2 * D_A], w_in[:, bcol:bcol + 2 * D_B]], axis=1).astype(bf)
    wv = jnp.concatenate([w_in[:, 2 * D_A:fcol], w_in[:, bcol + 2 * D_B:]], axis=1).astype(bf)
    wf = jnp.pad(w_in[:, fcol:bcol], ((0, 0), (0, LANES - N_HEADS_A))).astype(bf)
    b_row = jnp.pad(b_forget, (0, LANES - N_HEADS_A)).reshape(1, LANES)

    qk, vt, kaug = _proj(x, wqk, wv, wf, b_row)
    ya = _fox(qk, kaug, vt)
    yb = _band(qk, vt, _band_bias(rel_bias))
    return _ffn(ya, yb, x, w_out.astype(bf), ln1_g.reshape(1, d), ln1_b.reshape(1, d),
                w_up.astype(bf), conv_w, conv_b.reshape(1, -1), w_down.astype(bf),
                ln2_g.reshape(1, d), ln2_b.reshape(1, d))


def kernel(x, w_in, b_forget, rel_bias, w_out, ln1_g, ln1_b, w_up, conv_w, conv_b, w_down, ln2_g, ln2_b):
    for l in range(DEPTH):
        x = _layer(x, w_in[l], b_forget[l], rel_bias[l], w_out[l], ln1_g[l], ln1_b[l], w_up[l],
                   conv_w[l], conv_b[l], w_down[l], ln2_g[l], ln2_b[l])
    return x
```

```python
import math

import jax
import jax.numpy as jnp
import numpy as np
from jax import lax
from jax.experimental import pallas as pl
from jax.experimental.pallas import tpu as pltpu

D_MODEL = 1024
HEAD_DIM = 64
N_HEADS_A = 8
N_HEADS_B = 8
D_A = N_HEADS_A * HEAD_DIM
D_B = N_HEADS_B * HEAD_DIM
CHUNK = 64
N_LEFT_CHUNKS = 8
REL_CLIP = 128
D_FF = 2816
LN_EPS = 1e-5
DEPTH = 1
DEEPNORM_ALPHA = (2.0 * DEPTH) ** 0.25

LANES = 128
PAIR = 2 * HEAD_DIM
N_PAIRS_A = D_A // PAIR
N_PAIRS_B = D_B // PAIR
ONES_ROWS = 16
NEG = -1e30
LOG2E = math.log2(math.e)
VMEM_LIMIT = 52 * 1024 * 1024

PROJ_TM = 512
FOX_TQ = 2048
FOX_TK = 512
FOX_STRIP = 256
BAND_HALO = N_LEFT_CHUNKS * CHUNK
BAND_TQ = 2048
BAND_G = 256
BAND_KW = BAND_G + BAND_HALO
BAND_SKEW = 2
FFN_TM = 512
FFN_CW = 256


def _params(sem):
    return pltpu.CompilerParams(dimension_semantics=sem, vmem_limit_bytes=VMEM_LIMIT)


def _split3(v):
    h1 = v.astype(jnp.bfloat16).astype(jnp.float32)
    r1 = v - h1
    h2 = r1.astype(jnp.bfloat16).astype(jnp.float32)
    h3 = r1 - h2
    return h1, h2, h3


def _proj_kernel(x_ref, wqk_ref, wv_ref, wf_ref, b_ref, tri_ref, place_ref, qk_ref, vt_ref, ka_ref, carry_ref):
    i = pl.program_id(1)

    @pl.when(i == 0)
    def _():
        carry_ref[...] = jnp.zeros_like(carry_ref)

    xb = x_ref[...].astype(jnp.bfloat16)
    tm = xb.shape[0]
    z = jnp.dot(xb, wf_ref[...], preferred_element_type=jnp.float32) + b_ref[...]
    logf = -(jnp.maximum(-z, 0.0) + jnp.log1p(jnp.exp(-jnp.abs(z))))
    parts = jnp.concatenate(_split3(logf), axis=1).astype(jnp.bfloat16)

    qk = jnp.dot(xb, wqk_ref[...], preferred_element_type=jnp.float32)
    scale = HEAD_DIM ** -0.5 * LOG2E
    col = lax.broadcasted_iota(jnp.int32, (1, qk.shape[1]), 1)
    is_q = (col < D_A) | ((col >= 2 * D_A) & (col < 2 * D_A + D_B))
    qk_ref[...] = (qk * jnp.where(is_q, scale, 1.0)).astype(jnp.bfloat16)

    sums = jnp.dot(tri_ref[...], parts, preferred_element_type=jnp.float32)
    cum = carry_ref[0:1, :] + (sums[:, 0:LANES] + sums[:, LANES:2 * LANES] + sums[:, 2 * LANES:])
    carry_ref[...] = jnp.broadcast_to(cum[tm - 1:tm, :], carry_ref.shape)
    neg = cum * -LOG2E
    nparts = jnp.concatenate(_split3(neg), axis=1).astype(jnp.bfloat16)

    v = jnp.dot(xb, wv_ref[...], preferred_element_type=jnp.float32)
    vt_ref[...] = v.T.astype(jnp.bfloat16)

    ka = jnp.dot(nparts, place_ref[...], preferred_element_type=jnp.float32).astype(jnp.bfloat16)
    for hp in range(N_PAIRS_A):
        ka_ref[hp] = ka[:, hp * LANES:(hp + 1) * LANES]


def _gate_constants():
    tri = np.tril(np.ones((PROJ_TM, PROJ_TM), np.float32))
    place = np.zeros((3 * LANES, N_PAIRS_A * LANES), np.float32)
    for hp in range(N_PAIRS_A):
        for j in range(2):
            for t in range(3):
                place[t * LANES + 2 * hp + j, hp * LANES + 3 * j + t] = 1.0
    return jnp.asarray(tri, jnp.bfloat16), jnp.asarray(place, jnp.bfloat16)


def _proj(x, wqk, wv, wf, b_row):
    b, s, d = x.shape
    tm = PROJ_TM
    nqk = wqk.shape[1]
    nv = wv.shape[1]
    tri, place = _gate_constants()
    const = lambda arr: pl.BlockSpec(arr.shape, lambda bi, i: (0, 0))
    return pl.pallas_call(
        _proj_kernel,
        grid=(b, s // tm),
        in_specs=[
            pl.BlockSpec((None, tm, d), lambda bi, i: (bi, i, 0)),
            const(wqk), const(wv), const(wf), const(b_row), const(tri), const(place),
        ],
        out_specs=[
            pl.BlockSpec((None, tm, nqk), lambda bi, i: (bi, i, 0)),
            pl.BlockSpec((None, nv, tm), lambda bi, i: (bi, 0, i)),
            pl.BlockSpec((None, N_PAIRS_A, tm, LANES), lambda bi, i: (bi, 0, i, 0)),
        ],
        out_shape=[
            jax.ShapeDtypeStruct((b, s, nqk), jnp.bfloat16),
            jax.ShapeDtypeStruct((b, nv, s), jnp.bfloat16),
            jax.ShapeDtypeStruct((b, N_PAIRS_A, s, LANES), jnp.bfloat16),
        ],
        scratch_shapes=[pltpu.VMEM((8, LANES), jnp.float32)],
        compiler_params=_params(("parallel", "arbitrary")),
        name="proj",
    )(x, wqk, wv, wf, b_row, tri, place)


def _finalize(acc_e, acc_o):
    oe = acc_e[0:HEAD_DIM, :] / acc_e[HEAD_DIM:HEAD_DIM + 1, :]
    oo = acc_o[0:HEAD_DIM, :] / acc_o[HEAD_DIM:HEAD_DIM + 1, :]
    return jnp.concatenate([oe, oo], axis=0).T


def _vt_with_ones(vt_pair, h):
    ones_rows = jnp.ones((ONES_ROWS, vt_pair.shape[1]), jnp.bfloat16)
    return jnp.concatenate([vt_pair[h * HEAD_DIM:(h + 1) * HEAD_DIM, :], ones_rows], axis=0)


def _fox_kernel(q_ref, k_ref, ka_ref, vt_ref, o_ref, qe_ref, qo_ref, sa_ref, sb_ref, ma_ref, mb_ref,
                acc_e, acc_o, m_e, m_o):
    qi = pl.program_id(2)
    tq, tk = FOX_TQ, FOX_TK
    row = lax.broadcasted_iota(jnp.int32, (LANES, tq), 0)
    qt = q_ref[...].astype(jnp.float32).T
    qe_ref[0:LANES, :] = jnp.where(row < HEAD_DIM, qt, 0.0).astype(jnp.bfloat16)
    qo_ref[0:LANES, :] = jnp.where(row < HEAD_DIM, 0.0, qt).astype(jnp.bfloat16)
    qe_ref[LANES:, :] = jnp.where(row < 3, 1.0, 0.0).astype(jnp.bfloat16)
    qo_ref[LANES:, :] = jnp.where((row >= 3) & (row < 6), 1.0, 0.0).astype(jnp.bfloat16)
    acc_e[...] = jnp.zeros_like(acc_e)
    acc_o[...] = jnp.zeros_like(acc_o)
    m_e[...] = jnp.full_like(m_e, NEG)
    m_o[...] = jnp.full_like(m_o, NEG)

    q_refs = (qe_ref, qo_ref)
    stats = ((acc_e, m_e), (acc_o, m_o))

    def key_operands(kb):
        ks = pl.multiple_of(kb * tk, tk)
        kcat = jnp.concatenate([k_ref[pl.ds(ks, tk), :], ka_ref[pl.ds(ks, tk), :]], axis=1)
        return kcat, vt_ref[:, pl.ds(ks, tk)]

    def score_strip(kcat, h, c, s_ref, mx_ref):
        cs = slice(c, c + FOX_STRIP)
        st = jnp.dot(kcat, q_refs[h][:, cs], preferred_element_type=jnp.float32)
        s_ref[h, :, cs] = st
        mx_ref[h, :, cs] = jnp.max(st, axis=0, keepdims=True)

    tri_r = lax.broadcasted_iota(jnp.int32, (FOX_STRIP, FOX_STRIP), 0)
    tri_c = lax.broadcasted_iota(jnp.int32, (FOX_STRIP, FOX_STRIP), 1)
    tri = jnp.where(tri_r <= tri_c, 0.0, NEG)

    def update_strip(vt_h, h, c, s_ref, mx_ref, mask_c0=None):
        acc, m = stats[h]
        cs = slice(c, c + FOX_STRIP)
        if mask_c0 is None:
            st = s_ref[h, :, cs]
            blk_max = mx_ref[h, :, cs]
        else:
            d = c - mask_c0
            diag = s_ref[h, d:d + FOX_STRIP, cs] + tri
            st = diag if d == 0 else jnp.concatenate([s_ref[h, 0:d, cs], diag], axis=0)
            vt_h = vt_h[:, 0:d + FOX_STRIP]
            blk_max = jnp.max(st, axis=0, keepdims=True)
        m_old = m[:, cs]
        m_new = jnp.maximum(m_old, blk_max)
        p = jnp.exp2(st - m_new).astype(jnp.bfloat16)
        pv = jnp.dot(vt_h, p, preferred_element_type=jnp.float32)
        acc[:, cs] = acc[:, cs] * jnp.exp2(m_old - m_new) + pv
        m[:, cs] = m_new

    def scores(kb, s_ref, mx_ref, c0=0, ncol=tq):
        kcat, _ = key_operands(kb)
        for h in range(2):
            for c in range(c0, c0 + ncol, FOX_STRIP):
                score_strip(kcat, h, c, s_ref, mx_ref)

    def step(kb_next, s_next, mx_next, kb_cur, s_cur, mx_cur):
        kcat, _ = key_operands(kb_next)
        _, vt_pair = key_operands(kb_cur)
        for h in range(2):
            vt_h = _vt_with_ones(vt_pair, h)
            for c in range(0, tq, FOX_STRIP):
                score_strip(kcat, h, c, s_next, mx_next)
                update_strip(vt_h, h, c, s_cur, mx_cur)

    n_own = tq // tk
    kb0 = n_own * qi
    scores(0, sa_ref, ma_ref)

    def pair(kb):
        step(kb + 1, sb_ref, mb_ref, kb, sa_ref, ma_ref)
        step(kb + 2, sa_ref, ma_ref, kb + 1, sb_ref, mb_ref)

    def body(j, carry):
        pair(4 * j)
        pair(4 * j + 2)
        return carry

    n_pairs = kb0 // 2
    lax.fori_loop(0, n_pairs // 2, body, 0)
    if n_own % 4:
        @pl.when(n_pairs % 2 == 1)
        def _():
            pair(kb0 - 2)
    bufs = ((sa_ref, ma_ref), (sb_ref, mb_ref))
    for j in range(n_own):
        s_cur, mx_cur = bufs[j % 2]
        s_nxt, mx_nxt = bufs[(j + 1) % 2]
        _, vt_pair = key_operands(kb0 + j)
        kcat_nxt = key_operands(kb0 + j + 1)[0] if j + 1 < n_own else None
        for h in range(2):
            vt_h = _vt_with_ones(vt_pair, h)
            for c in range((j + 1) * tk, tq, FOX_STRIP):
                score_strip(kcat_nxt, h, c, s_nxt, mx_nxt)
                update_strip(vt_h, h, c, s_cur, mx_cur)
            for c in range(j * tk, (j + 1) * tk, FOX_STRIP):
                update_strip(vt_h, h, c, s_cur, mx_cur, mask_c0=j * tk)
    o_ref[...] = _finalize(acc_e[...], acc_o[...]).astype(o_ref.dtype)


def _fox(qk, kaug, vt):
    b, s, _ = qk.shape
    tq, tk = FOX_TQ, FOX_TK
    kcol0 = D_A // LANES
    return pl.pallas_call(
        _fox_kernel,
        grid=(b, N_PAIRS_A, s // tq),
        in_specs=[
            pl.BlockSpec((None, tq, LANES), lambda bi, hp, qi: (bi, qi, hp)),
            pl.BlockSpec((None, s, LANES), lambda bi, hp, qi: (bi, 0, kcol0 + hp)),
            pl.BlockSpec((None, None, s, LANES), lambda bi, hp, qi: (bi, hp, 0, 0)),
            pl.BlockSpec((None, LANES, s), lambda bi, hp, qi: (bi, hp, 0)),
        ],
        out_specs=pl.BlockSpec((None, tq, LANES), lambda bi, hp, qi: (bi, qi, hp)),
        out_shape=jax.ShapeDtypeStruct((b, s, D_A), jnp.bfloat16),
        scratch_shapes=[
            pltpu.VMEM((2 * LANES, tq), jnp.bfloat16),
            pltpu.VMEM((2 * LANES, tq), jnp.bfloat16),
            pltpu.VMEM((2, tk, tq), jnp.float32),
            pltpu.VMEM((2, tk, tq), jnp.float32),
            pltpu.VMEM((2, 1, tq), jnp.float32),
            pltpu.VMEM((2, 1, tq), jnp.float32),
            pltpu.VMEM((HEAD_DIM + ONES_ROWS, tq), jnp.float32),
            pltpu.VMEM((HEAD_DIM + ONES_ROWS, tq), jnp.float32),
            pltpu.VMEM((1, tq), jnp.float32),
            pltpu.VMEM((1, tq), jnp.float32),
        ],
        compiler_params=_params(("parallel", "parallel", "arbitrary")),
        name="fox",
    )(qk, qk, kaug, vt)


def _band_kernel(q_ref, kp_ref, kc_ref, vtp_ref, vtc_ref, bias_ref, o_ref, kwin, vtwin, s_scr, p_scr):
    i = pl.program_id(2)
    tq, halo = BAND_TQ, BAND_HALO
    n_units = 2 * (tq // BAND_G)
    lane = lax.broadcasted_iota(jnp.int32, (halo, LANES), 1)
    kwin[0:halo, 0:LANES] = kp_ref[...]
    kwin[halo:, 0:LANES] = kc_ref[...]
    pen = jnp.where(i == 0, NEG, 0.0)
    kwin[0:halo, LANES:] = jnp.where(lane == 0, pen, 0.0).astype(jnp.bfloat16)
    kwin[halo:, LANES:] = jnp.zeros((tq, LANES), jnp.bfloat16)
    vtwin[:, 0:halo] = vtp_ref[...]
    vtwin[:, halo:] = vtc_ref[...]
    row_g = lax.broadcasted_iota(jnp.int32, (LANES, BAND_G), 0)
    ones_row = jnp.where(row_g == 0, 1.0, 0.0).astype(jnp.bfloat16)

    def scores(u):
        g, h = divmod(u, 2)
        qt = q_ref[g * BAND_G:(g + 1) * BAND_G, :].astype(jnp.float32).T
        own = (row_g < HEAD_DIM) if h == 0 else (row_g >= HEAD_DIM)
        qcat = jnp.concatenate([jnp.where(own, qt, 0.0).astype(jnp.bfloat16), ones_row], axis=0)
        kslab = kwin[g * BAND_G:g * BAND_G + BAND_KW, :]
        s_scr[u] = jnp.dot(kslab, qcat, preferred_element_type=jnp.float32)

    def softmax(u):
        st = s_scr[u] + bias_ref[u % 2]
        m = jnp.max(st, axis=0, keepdims=True)
        p_scr[u] = jnp.exp2(st - m).astype(jnp.bfloat16)

    accs = {}

    def pv(u):
        g, h = divmod(u, 2)
        vslab = vtwin[:, g * BAND_G:g * BAND_G + BAND_KW]
        accs[h] = jnp.dot(_vt_with_ones(vslab, h), p_scr[u], preferred_element_type=jnp.float32)
        if h == 1:
            o_ref[g * BAND_G:(g + 1) * BAND_G, :] = _finalize(accs[0], accs[1]).astype(o_ref.dtype)

    for t in range(n_units + 2 * BAND_SKEW):
        if t < n_units:
            scores(t)
        if BAND_SKEW <= t < n_units + BAND_SKEW:
            softmax(t - BAND_SKEW)
        if t >= 2 * BAND_SKEW:
            pv(t - 2 * BAND_SKEW)


def _band(qk, vt, bias_t):
    b, s, _ = qk.shape
    tq, halo = BAND_TQ, BAND_HALO
    r = tq // halo
    qcol0 = 2 * D_A // LANES
    kcol0 = (2 * D_A + D_B) // LANES
    vrow0 = D_A // PAIR
    prev = lambda i: jnp.maximum(i * r - 1, 0)
    n_units = 2 * (tq // BAND_G)
    return pl.pallas_call(
        _band_kernel,
        grid=(b, N_PAIRS_B, s // tq),
        in_specs=[
            pl.BlockSpec((None, tq, LANES), lambda bi, hp, i: (bi, i, qcol0 + hp)),
            pl.BlockSpec((None, halo, LANES), lambda bi, hp, i: (bi, prev(i), kcol0 + hp)),
            pl.BlockSpec((None, tq, LANES), lambda bi, hp, i: (bi, i, kcol0 + hp)),
            pl.BlockSpec((None, PAIR, halo), lambda bi, hp, i: (bi, vrow0 + hp, prev(i))),
            pl.BlockSpec((None, PAIR, tq), lambda bi, hp, i: (bi, vrow0 + hp, i)),
            pl.BlockSpec((2, BAND_KW, BAND_G), lambda bi, hp, i: (hp, 0, 0)),
        ],
        out_specs=pl.BlockSpec((None, tq, LANES), lambda bi, hp, i: (bi, i, hp)),
        out_shape=jax.ShapeDtypeStruct((b, s, D_B), jnp.bfloat16),
        scratch_shapes=[
            pltpu.VMEM((halo + tq, 2 * LANES), jnp.bfloat16),
            pltpu.VMEM((PAIR, halo + tq), jnp.bfloat16),
            pltpu.VMEM((n_units, BAND_KW, BAND_G), jnp.float32),
            pltpu.VMEM((n_units, BAND_KW, BAND_G), jnp.bfloat16),
        ],
        compiler_params=_params(("parallel", "parallel", "arbitrary")),
        name="band",
    )(qk, qk, qk, vt, vt, bias_t)


BIAS_W = 1024


def _bias_kernel(g_ref, o_ref):
    row = jnp.broadcast_to(g_ref[...], (BAND_KW, BIAS_W))
    toe = pltpu.roll(row, 0, 1, stride=1, stride_axis=0)[:, 0:BAND_G]
    kc = lax.broadcasted_iota(jnp.int32, (BAND_KW, BAND_G), 0) // CHUNK
    qc = lax.broadcasted_iota(jnp.int32, (BAND_KW, BAND_G), 1) // CHUNK
    inband = (kc >= qc) & (kc <= qc + N_LEFT_CHUNKS)
    o_ref[...] = jnp.where(inband, toe * LOG2E, NEG)


def _band_bias(rel_bias):
    h = rel_bias.shape[0]
    j = np.arange(BIAS_W)
    off = np.where(j < BAND_G, j, j - BIAS_W)
    idx = np.clip(N_LEFT_CHUNKS * CHUNK + off, -REL_CLIP, REL_CLIP) + REL_CLIP
    g_ext = rel_bias.astype(jnp.float32)[:, idx].reshape(h, 1, BIAS_W)
    return pl.pallas_call(
        _bias_kernel,
        grid=(h,),
        in_specs=[pl.BlockSpec((None, 1, BIAS_W), lambda i: (i, 0, 0))],
        out_specs=pl.BlockSpec((None, BAND_KW, BAND_G), lambda i: (i, 0, 0)),
        out_shape=jax.ShapeDtypeStruct((h, BAND_KW, BAND_G), jnp.float32),
        compiler_params=_params(("parallel",)),
        name="bias",
    )(g_ext)


def _layer_norm(z, g, b):
    mu = jnp.mean(z, axis=-1, keepdims=True)
    zc = z - mu
    var = jnp.mean(zc * zc, axis=-1, keepdims=True)
    return zc * lax.rsqrt(var + LN_EPS) * g + b


def _gelu_tanh(x):
    return 0.5 * x * (1.0 + jnp.tanh(math.sqrt(2.0 / math.pi) * (x + 0.044715 * (x * x * x))))


def _causal_conv(u, carry, w, bias):
    tm = u.shape[0]
    r8 = lax.broadcasted_iota(jnp.int32, (8, u.shape[1]), 0)
    outs = w[2:3, :] * u + bias
    for shift, j in ((1, 1), (2, 0)):
        ur = pltpu.roll(u, shift, 0)
        cr = pltpu.roll(carry, shift, 0)
        head = jnp.where(r8 < shift, cr, ur[0:8, :])
        us = jnp.concatenate([head, ur[8:tm, :]], axis=0)
        outs = outs + w[j:j + 1, :] * us
    return outs


def _ffn_kernel(ya_ref, yb_ref, x_ref, wout_ref, g1_ref, b1_ref, wup_ref, cw_ref, cb_ref, wdn_ref, g2_ref, b2_ref,
                o_ref, carry_ref, h_ref):
    i = pl.program_id(1)

    @pl.when(i == 0)
    def _():
        carry_ref[...] = jnp.zeros_like(carry_ref)

    hm = x_ref.shape[0] // 2
    row_halves = (slice(0, hm), slice(hm, 2 * hm))
    x1 = []
    for r in row_halves:
        y = jnp.concatenate([ya_ref[r, :], yb_ref[r, :]], axis=1)
        mix = jnp.dot(y, wout_ref[...], preferred_element_type=jnp.float32)
        x1.append(_layer_norm(DEEPNORM_ALPHA * x_ref[r, :] + mix, g1_ref[...], b1_ref[...]))
    for r, x1h in zip(row_halves, x1):
        xb = x1h.astype(jnp.bfloat16)
        for c in range(D_FF // FFN_CW):
            halves = []
            for off in (c * FFN_CW, D_FF + c * FFN_CW):
                sl = slice(off, off + FFN_CW)
                u = jnp.dot(xb, wup_ref[:, sl], preferred_element_type=jnp.float32)
                halves.append(_causal_conv(u, carry_ref[:, sl], cw_ref[:, sl], cb_ref[:, sl]))
                carry_ref[:, sl] = u[hm - 8:hm, :]
            h_ref[r, c * FFN_CW:(c + 1) * FFN_CW] = (halves[0] * _gelu_tanh(halves[1])).astype(jnp.bfloat16)
    for r, x1h in zip(row_halves, x1):
        ffn = jnp.dot(h_ref[r, :], wdn_ref[...], preferred_element_type=jnp.float32)
        o_ref[r, :] = _layer_norm(DEEPNORM_ALPHA * x1h + ffn, g2_ref[...], b2_ref[...])


def _ffn(ya, yb, x, w_out, g1, b1, w_up, conv_w, conv_b, w_down, g2, b2):
    bsz, s, d = x.shape
    tm = FFN_TM
    nu = w_up.shape[1]
    const = lambda shape: pl.BlockSpec(shape, lambda bi, i: (0, 0), pipeline_mode=pl.Buffered(1))
    rows = lambda width: pl.BlockSpec((None, tm, width), lambda bi, i: (bi, i, 0))
    return pl.pallas_call(
        _ffn_kernel,
        grid=(bsz, s // tm),
        in_specs=[
            rows(ya.shape[2]), rows(yb.shape[2]), rows(d),
            const((d, d)), const((1, d)), const((1, d)),
            const((d, nu)), const((3, nu)), const((1, nu)), const((D_FF, d)), const((1, d)), const((1, d)),
        ],
        out_specs=rows(d),
        out_shape=jax.ShapeDtypeStruct((bsz, s, d), jnp.float32),
        scratch_shapes=[
            pltpu.VMEM((8, nu), jnp.float32),
            pltpu.VMEM((tm, D_FF), jnp.bfloat16),
        ],
        compiler_params=_params(("parallel", "arbitrary")),
        name="ffn",
    )(ya, yb, x, w_out, g1, b1, w_up, conv_w, conv_b, w_down, g2, b2)


def _layer(x, w_in, b_forget, rel_bias, w_out, ln1_g, ln1_b, w_up, conv_w, conv_b, w_down, ln2_g, ln2_b):
    b, s, d = x.shape
    bf = jnp.bfloat16
    fcol = 3 * D_A
    bcol = fcol + N_HEADS_A
    wqk = jnp.concatenate([w_in[:, 0:2 * D_A], w_in[:, bcol:bcol + 2 * D_B]], axis=1).astype(bf)
    wv = jnp.concatenate([w_in[:, 2 * D_A:fcol], w_in[:, bcol + 2 * D_B:]], axis=1).astype(bf)
    wf = jnp.pad(w_in[:, fcol:bcol], ((0, 0), (0, LANES - N_HEADS_A))).astype(bf)
    b_row = jnp.pad(b_forget, (0, LANES - N_HEADS_A)).reshape(1, LANES)

    qk, vt, kaug = _proj(x, wqk, wv, wf, b_row)
    ya = _fox(qk, kaug, vt)
    yb = _band(qk, vt, _band_bias(rel_bias))
    return _ffn(ya, yb, x, w_out.astype(bf), ln1_g.reshape(1, d), ln1_b.reshape(1, d),
                w_up.astype(bf), conv_w, conv_b.reshape(1, -1), w_down.astype(bf),
                ln2_g.reshape(1, d), ln2_b.reshape(1, d))


def kernel(x, w_in, b_forget, rel_bias, w_out, ln1_g, ln1_b, w_up, conv_w, conv_b, w_down, ln2_g, ln2_b):
    for l in range(DEPTH):
        x = _layer(x, w_in[l], b_forget[l], rel_bias[l], w_out[l], ln1_g[l], ln1_b[l], w_up[l],
                   conv_w[l], conv_b[l], w_down[l], ln2_g[l], ln2_b[l])
    return x
```
